```python
import jax, jax.numpy as jnp
from jax import lax
import numpy as np

D_MODEL = 1024
BATCH = 8
SEQ = 8192
DEPTH = 1

HEAD_DIM = 128
GDN_HEADS = 4
GDN_QK = GDN_HEADS * HEAD_DIM
GDN_V = GDN_HEADS * HEAD_DIM
GDN_CONV = 4
GDN_CHUNK = 64
NSA_HEADS = 4
NSA_KV_GROUPS = 1
NSA_HPG = NSA_HEADS // NSA_KV_GROUPS
NSA_Q = NSA_HEADS * HEAD_DIM
NSA_KV = NSA_KV_GROUPS * HEAD_DIM
CMP_LEN = 32
CMP_STRIDE = 16
SLC_BLOCK = 64
SLC_TOPK = 16
WINDOW = 512
NSA_Q_BLOCK = 64
FORCE_SCORE = 1.0e4
MEM_LEN = 256
XA_HEADS = 4
XA_HEAD_DIM = D_MODEL // XA_HEADS
D_FF = ((8 * D_MODEL // 3 + 255) // 256) * 256
ROPE_THETA = 10000.0
LN_EPS = 1e-5
RMS_EPS = 1e-6
NEG = -1.0e30
ATTN_SCALE = HEAD_DIM ** -0.5
DN_ALPHA = (2 * DEPTH) ** 0.25
DN_BETA = (8 * DEPTH) ** -0.25
IN_SPLITS = (3 * GDN_QK, GDN_V, GDN_HEADS, GDN_HEADS, NSA_Q, 6 * NSA_KV, 3 * NSA_HEADS, D_MODEL, D_MODEL)
N_IN = sum(IN_SPLITS)

kernel_name = "hybrid_gdn_nsa_deepnorm_block"


def layer_norm(x, g, b):
    xf = x.astype(jnp.float32)
    mu = jnp.mean(xf, -1, keepdims=True)
    var = jnp.mean(jnp.square(xf - mu), -1, keepdims=True)
    return ((xf - mu) * lax.rsqrt(var + LN_EPS) * g + b).astype(x.dtype)


def l2norm(x):
    xf = x.astype(jnp.float32)
    return xf * lax.rsqrt(jnp.sum(xf * xf, -1, keepdims=True) + RMS_EPS)


def rope(x, pos):
    half = x.shape[-1] // 2
    inv = ROPE_THETA ** (-jnp.arange(half, dtype=jnp.float32) / half)
    ang = pos.astype(jnp.float32)[..., None] * inv
    cos = jnp.cos(ang)[:, :, None, :]
    sin = jnp.sin(ang)[:, :, None, :]
    xf = x.astype(jnp.float32)
    x1, x2 = xf[..., :half], xf[..., half:]
    return jnp.concatenate([x1 * cos - x2 * sin, x2 * cos + x1 * sin], -1).astype(x.dtype)


def causal_dwconv(x, w):
    k, c = w.shape
    return lax.conv_general_dilated(x, w[:, None, :].astype(x.dtype), window_strides=(1,),
                                    padding=[(k - 1, 0)], dimension_numbers=('NWC', 'WIO', 'NWC'),
                                    feature_group_count=c)


def gated_delta_rule(q, k, v, g, beta):
    bn, h, s, dk = q.shape
    dv = v.shape[-1]
    c = GDN_CHUNK
    n = s // c
    q, k, v = (t.reshape(bn, h, n, c, t.shape[-1]) for t in (q, k, v))
    g, beta = (t.reshape(bn, h, n, c) for t in (g, beta))
    gc = jnp.cumsum(g, axis=-1)
    incl = jnp.tril(jnp.ones((c, c), bool))
    strict = jnp.tril(jnp.ones((c, c), bool), -1)
    decay = jnp.exp(jnp.where(incl, gc[..., :, None] - gc[..., None, :], -jnp.inf))
    k_beta = k * beta[..., None]
    a_mat = jnp.where(strict, jnp.einsum('bhnid,bhnjd->bhnij', k_beta, k) * decay, 0.0)
    eye = jnp.eye(c, dtype=a_mat.dtype)
    t_inv = lax.linalg.triangular_solve(eye + a_mat, jnp.broadcast_to(eye, a_mat.shape),
                                        left_side=True, lower=True, unit_diagonal=True)
    u = jnp.einsum('bhnij,bhnjd->bhnid', t_inv, v * beta[..., None])
    w = jnp.einsum('bhnij,bhnjd->bhnid', t_inv, k_beta * jnp.exp(gc)[..., None])
    attn = jnp.einsum('bhnid,bhnjd->bhnij', q, k) * decay
    q_dec = q * jnp.exp(gc)[..., None]
    g_last = gc[..., -1]
    k_tail = k * jnp.exp(g_last[..., None] - gc)[..., None]

    def step(state, inp):
        q_c, u_c, w_c, a_c, kt_c, gl_c = inp
        v_new = u_c - jnp.einsum('bhid,bhde->bhie', w_c, state)
        o_c = jnp.einsum('bhid,bhde->bhie', q_c, state) + jnp.einsum('bhij,bhje->bhie', a_c, v_new)
        state = state * jnp.exp(gl_c)[..., None, None] + jnp.einsum('bhid,bhie->bhde', kt_c, v_new)
        return state, o_c

    xs = tuple(jnp.moveaxis(t, 2, 0) for t in (q_dec, u, w, attn, k_tail, g_last))
    state0 = jnp.zeros((bn, h, dk, dv), q.dtype)
    _, o = lax.scan(step, state0, xs)
    return jnp.moveaxis(o, 0, 2).reshape(bn, h, s, dv)


def compress(t, pos_emb, w1, w2):
    bn, s, g, dh = t.shape
    r = CMP_LEN // CMP_STRIDE
    n_cmp = s // CMP_STRIDE - r + 1
    seg = t.reshape(bn, s // CMP_STRIDE, CMP_STRIDE, g, dh)
    blk = jnp.concatenate([seg[:, i:i + n_cmp] for i in range(r)], axis=2)
    blk = blk + pos_emb[:, None, :]
    flat = blk.transpose(0, 1, 3, 2, 4).reshape(bn, n_cmp, g, CMP_LEN * dh)
    return jax.nn.silu(flat @ w1) @ w2


def nsa_attention(q, kv6, gates, positions, cmp_pos_k, cmp_pos_v, cmp_k_w1, cmp_k_w2, cmp_v_w1, cmp_v_w2):
    bn, s, _ = q.shape
    g_, hpg, dh = NSA_KV_GROUPS, NSA_HPG, HEAD_DIM
    q = rope(q.reshape(bn, s, NSA_HEADS, dh), positions)
    q = q.reshape(bn, s, g_, hpg, dh).transpose(0, 2, 3, 1, 4)
    kv6 = kv6.reshape(bn, s, 6, g_, dh)
    k_cmp_raw, v_cmp_raw, k_slc, v_slc, k_win, v_win = (kv6[:, :, i] for i in range(6))
    n_cmp = s // CMP_STRIDE - CMP_LEN // CMP_STRIDE + 1
    n_slc = s // SLC_BLOCK
    top_k = min(SLC_TOPK, n_slc)
    cmp_pos = positions[:, CMP_LEN - 1::CMP_STRIDE][:, :n_cmp]
    k_cmp = rope(compress(k_cmp_raw, cmp_pos_k, cmp_k_w1, cmp_k_w2), cmp_pos).transpose(0, 2, 1, 3)
    v_cmp = compress(v_cmp_raw, cmp_pos_v, cmp_v_w1, cmp_v_w2).transpose(0, 2, 1, 3)
    cmp_start = jnp.arange(n_cmp) * CMP_STRIDE
    cmp_end = cmp_start + CMP_LEN - 1
    slc_start = jnp.arange(n_slc) * SLC_BLOCK
    overlap = ((cmp_start[:, None] < slc_start[None, :] + SLC_BLOCK) &
               (cmp_start[:, None] + CMP_LEN > slc_start[None, :])).astype(jnp.float32)
    k_slc = rope(k_slc, positions).transpose(0, 2, 1, 3).reshape(bn, g_, n_slc, SLC_BLOCK * dh)
    v_slc = v_slc.transpose(0, 2, 1, 3).reshape(bn, g_, n_slc, SLC_BLOCK * dh)
    pad = ((0, 0), (0, 0), (WINDOW, 0), (0, 0))
    k_win = jnp.pad(rope(k_win, positions).transpose(0, 2, 1, 3), pad)
    v_win = jnp.pad(v_win.transpose(0, 2, 1, 3), pad)
    gates = jax.nn.sigmoid(gates.astype(jnp.float32)).reshape(bn, s, g_, hpg, 3).transpose(0, 2, 3, 1, 4)
    sel_shape = (bn, g_, NSA_Q_BLOCK, top_k, SLC_BLOCK, dh)

    def block(qb):
        qs = qb * NSA_Q_BLOCK
        t = qs + jnp.arange(NSA_Q_BLOCK)
        qblk = lax.dynamic_slice_in_dim(q, qs, NSA_Q_BLOCK, axis=3)
        gblk = lax.dynamic_slice_in_dim(gates, qs, NSA_Q_BLOCK, axis=3)
        s_c = jnp.einsum('bghqd,bgnd->bghqn', qblk, k_cmp).astype(jnp.float32) * ATTN_SCALE
        valid_c = cmp_end[None, :] <= t[:, None]
        p_c = jnp.where(valid_c, jax.nn.softmax(jnp.where(valid_c, s_c, NEG), -1), 0.0)
        o_c = jnp.einsum('bghqn,bgnd->bghqd', p_c.astype(v_cmp.dtype), v_cmp)
        imp = jnp.einsum('bghqn,nj->bgqj', p_c, overlap)
        cur = t // SLC_BLOCK
        j = jnp.arange(n_slc)[None, :]
        forced = (j == 0) | (j == cur[:, None]) | (j == cur[:, None] - 1)
        imp = jnp.where(forced, FORCE_SCORE, jnp.where(j <= cur[:, None], imp, -1.0))
        _, idx = lax.top_k(imp, top_k)
        flat_idx = idx.reshape(bn, g_, NSA_Q_BLOCK * top_k, 1)
        k_sel = jnp.take_along_axis(k_slc, flat_idx, axis=2).reshape(sel_shape)
        v_sel = jnp.take_along_axis(v_slc, flat_idx, axis=2).reshape(sel_shape)
        s_s = jnp.einsum('bghqd,bgqkld->bghqkl', qblk, k_sel).astype(jnp.float32) * ATTN_SCALE
        key_pos = idx[..., None] * SLC_BLOCK + jnp.arange(SLC_BLOCK)
        mask_s = (key_pos <= t[:, None, None])[:, :, None]
        s_s = jnp.where(mask_s, s_s, NEG).reshape(bn, g_, hpg, NSA_Q_BLOCK, top_k * SLC_BLOCK)
        p_s = jax.nn.softmax(s_s, -1).reshape(bn, g_, hpg, NSA_Q_BLOCK, top_k, SLC_BLOCK)
        o_s = jnp.einsum('bghqkl,bgqkld->bghqd', p_s.astype(v_sel.dtype), v_sel)
        kw = lax.dynamic_slice_in_dim(k_win, qs, NSA_Q_BLOCK + WINDOW, axis=2)
        vw = lax.dynamic_slice_in_dim(v_win, qs, NSA_Q_BLOCK + WINDOW, axis=2)
        kpos = qs - WINDOW + jnp.arange(NSA_Q_BLOCK + WINDOW)
        rel = t[:, None] - kpos[None, :]
        mask_w = (rel >= 0) & (rel < WINDOW) & (kpos[None, :] >= 0)
        s_w = jnp.einsum('bghqd,bgkd->bghqk', qblk, kw).astype(jnp.float32) * ATTN_SCALE
        p_w = jax.nn.softmax(jnp.where(mask_w, s_w, NEG), -1)
        o_w = jnp.einsum('bghqk,bgkd->bghqd', p_w.astype(vw.dtype), vw)
        o = gblk[..., 0:1] * o_c + gblk[..., 1:2] * o_s + gblk[..., 2:3] * o_w
        return o.astype(q.dtype)

    o = lax.map(block, jnp.arange(s // NSA_Q_BLOCK))
    return o.transpose(1, 0, 4, 2, 3, 5).reshape(bn, s, NSA_Q)


def hybrid_mixer(x, positions, w_in, conv_w, a_log, dt_bias, gdn_norm_w, cmp_pos_k, cmp_pos_v,
                 cmp_k_w1, cmp_k_w2, cmp_v_w1, cmp_v_w2, w_up_gdn, w_up_nsa, w_mix_out):
    bn, s, _ = x.shape
    proj = x @ w_in
    split_pts = np.cumsum(IN_SPLITS)[:-1].tolist()
    qkv, z, a, b, q_n, kv6, g_n, gate_a, gate_b = jnp.split(proj, split_pts, axis=-1)
    qkv = jax.nn.silu(causal_dwconv(qkv, conv_w))
    qg, kg, vg = jnp.split(qkv, [GDN_QK, 2 * GDN_QK], axis=-1)
    heads = lambda t: t.reshape(bn, s, GDN_HEADS, HEAD_DIM).transpose(0, 2, 1, 3)
    qg = l2norm(heads(qg)) * (HEAD_DIM ** -0.5)
    kg = l2norm(heads(kg))
    vg = heads(vg).astype(jnp.float32)
    g = -jnp.exp(a_log.astype(jnp.float32)) * jax.nn.softplus(a.astype(jnp.float32) + dt_bias)
    beta = jax.nn.sigmoid(b.astype(jnp.float32))
    o_a = gated_delta_rule(qg, kg, vg, g.transpose(0, 2, 1), beta.transpose(0, 2, 1)).transpose(0, 2, 1, 3)
    zf = z.reshape(bn, s, GDN_HEADS, HEAD_DIM).astype(jnp.float32)
    o_a = o_a * lax.rsqrt(jnp.mean(o_a * o_a, -1, keepdims=True) + RMS_EPS) * gdn_norm_w * jax.nn.silu(zf)
    y_a = o_a.reshape(bn, s, GDN_V).astype(x.dtype) @ w_up_gdn
    o_b = nsa_attention(q_n, kv6, g_n, positions, cmp_pos_k, cmp_pos_v, cmp_k_w1, cmp_k_w2, cmp_v_w1, cmp_v_w2)
    y_b = o_b @ w_up_nsa
    y = jax.nn.sigmoid(gate_a) * y_a + jax.nn.sigmoid(gate_b) * y_b
    return y @ w_mix_out


def memory_cross_attention(x, mem, wq, wk, wv, wo):
    bn, s, _ = x.shape
    q = (x @ wq).reshape(bn, s, XA_HEADS, XA_HEAD_DIM)
    k = (mem @ wk).reshape(bn, -1, XA_HEADS, XA_HEAD_DIM)
    v = (mem @ wv).reshape(bn, -1, XA_HEADS, XA_HEAD_DIM)
    sc = jnp.einsum('bshd,bmhd->bhsm', q, k).astype(jnp.float32) * (XA_HEAD_DIM ** -0.5)
    p = jax.nn.softmax(sc, -1).astype(v.dtype)
    o = jnp.einsum('bhsm,bmhd->bshd', p, v).reshape(bn, s, D_MODEL)
    return o @ wo


def swiglu(x, w_gate, w_up, w_down):
    return (jax.nn.silu(x @ w_gate) * (x @ w_up)) @ w_down


def setup_inputs(seed: int = 0) -> dict:
    key = jax.random.key(seed)
    ks = iter(jax.random.split(key, 40))
    nrm = lambda shape, scale: jax.random.normal(next(ks), shape, jnp.float32) * scale
    L = DEPTH
    x = jax.random.normal(next(ks), (BATCH, SEQ, D_MODEL), jnp.float32)
    mem = jax.random.normal(next(ks), (BATCH, MEM_LEN, D_MODEL), jnp.float32)
    offs = jax.random.randint(next(ks), (BATCH, 1), 0, 4096, jnp.int32)
    positions = offs + jnp.arange(SEQ, dtype=jnp.int32)[None, :]
    dt = jnp.exp(jax.random.uniform(next(ks), (L, GDN_HEADS), jnp.float32, np.log(1e-3), np.log(1e-1)))
    return {
        "x": x,
        "mem": mem,
        "positions": positions,
        "w_in": nrm((L, D_MODEL, N_IN), D_MODEL ** -0.5),
        "gdn_conv_w": nrm((L, GDN_CONV, 3 * GDN_QK), GDN_CONV ** -0.5),
        "gdn_a_log": jnp.log(jax.random.uniform(next(ks), (L, GDN_HEADS), jnp.float32, 1.0, 16.0)),
        "gdn_dt_bias": dt + jnp.log(-jnp.expm1(-dt)),
        "gdn_norm_w": 1.0 + nrm((L, HEAD_DIM), 0.02),
        "cmp_pos_k": nrm((L, CMP_LEN, HEAD_DIM), 0.02),
        "cmp_pos_v": nrm((L, CMP_LEN, HEAD_DIM), 0.02),
        "cmp_k_w1": nrm((L, CMP_LEN * HEAD_DIM, HEAD_DIM), (CMP_LEN * HEAD_DIM) ** -0.5),
        "cmp_k_w2": nrm((L, HEAD_DIM, HEAD_DIM), HEAD_DIM ** -0.5),
        "cmp_v_w1": nrm((L, CMP_LEN * HEAD_DIM, HEAD_DIM), (CMP_LEN * HEAD_DIM) ** -0.5),
        "cmp_v_w2": nrm((L, HEAD_DIM, HEAD_DIM), HEAD_DIM ** -0.5),
        "w_up_gdn": nrm((L, GDN_V, D_MODEL), GDN_V ** -0.5),
        "w_up_nsa": nrm((L, NSA_Q, D_MODEL), NSA_Q ** -0.5),
        "w_mix_out": nrm((L, D_MODEL, D_MODEL), D_MODEL ** -0.5 * DN_BETA),
        "ln1_g": 1.0 + nrm((L, D_MODEL), 0.02),
        "ln1_b": nrm((L, D_MODEL), 0.02),
        "xa_wq": nrm((L, D_MODEL, D_MODEL), D_MODEL ** -0.5),
        "xa_wk": nrm((L, D_MODEL, D_MODEL), D_MODEL ** -0.5),
        "xa_wv": nrm((L, D_MODEL, D_MODEL), D_MODEL ** -0.5),
        "xa_wo": nrm((L, D_MODEL, D_MODEL), D_MODEL ** -0.5 * DN_BETA),
        "ln2_g": 1.0 + nrm((L, D_MODEL), 0.02),
        "ln2_b": nrm((L, D_MODEL), 0.02),
        "ffn_w_gate": nrm((L, D_MODEL, D_FF), D_MODEL ** -0.5),
        "ffn_w_up": nrm((L, D_MODEL, D_FF), D_MODEL ** -0.5),
        "ffn_w_down": nrm((L, D_FF, D_MODEL), D_FF ** -0.5 * DN_BETA),
        "ln3_g": 1.0 + nrm((L, D_MODEL), 0.02),
        "ln3_b": nrm((L, D_MODEL), 0.02),
    }


def reference(x, mem, positions, w_in, gdn_conv_w, gdn_a_log, gdn_dt_bias, gdn_norm_w, cmp_pos_k, cmp_pos_v,
              cmp_k_w1, cmp_k_w2, cmp_v_w1, cmp_v_w2, w_up_gdn, w_up_nsa, w_mix_out, ln1_g, ln1_b,
              xa_wq, xa_wk, xa_wv, xa_wo, ln2_g, ln2_b, ffn_w_gate, ffn_w_up, ffn_w_down, ln3_g, ln3_b):
    for l in range(DEPTH):
        y = hybrid_mixer(x, positions, w_in[l], gdn_conv_w[l], gdn_a_log[l], gdn_dt_bias[l], gdn_norm_w[l],
                         cmp_pos_k[l], cmp_pos_v[l], cmp_k_w1[l], cmp_k_w2[l], cmp_v_w1[l], cmp_v_w2[l],
                         w_up_gdn[l], w_up_nsa[l], w_mix_out[l])
        x = layer_norm(DN_ALPHA * x + y, ln1_g[l], ln1_b[l])
        y = memory_cross_attention(x, mem, xa_wq[l], xa_wk[l], xa_wv[l], xa_wo[l])
        x = layer_norm(DN_ALPHA * x + y, ln2_g[l], ln2_b[l])
        y = swiglu(x, ffn_w_gate[l], ffn_w_up[l], ffn_w_down[l])
        x = layer_norm(DN_ALPHA * x + y, ln3_g[l], ln3_b[l])
    return x
```

```python
import functools

import jax
import jax.numpy as jnp
from jax import lax
from jax.experimental import pallas as pl
from jax.experimental.pallas import tpu as pltpu

F32 = jnp.float32
BF16 = jnp.bfloat16

D_MODEL = 1024
HEAD_DIM = 128
GDN_HEADS = 4
GDN_QK = GDN_HEADS * HEAD_DIM
GDN_V = GDN_HEADS * HEAD_DIM
GDN_CONV = 4
GDN_CHUNK = 64
NSA_HEADS = 4
NSA_Q = NSA_HEADS * HEAD_DIM
CMP_LEN = 32
CMP_STRIDE = 16
SLC_BLOCK = 64
SLC_TOPK = 16
WINDOW = 512
FORCE_SCORE = 1.0e4
XA_HEADS = 4
XA_HEAD_DIM = D_MODEL // XA_HEADS
ROPE_THETA = 10000.0
LN_EPS = 1e-5
RMS_EPS = 1e-6
NEG = -1.0e30
ATTN_SCALE = HEAD_DIM ** -0.5
DEPTH = 1
DN_ALPHA = (2 * DEPTH) ** 0.25

LANES = 128
VMEM_LIMIT_BYTES = 56 * 1024 * 1024

COL_GATE_A = 0
COL_GATE_B = 1024
COL_GQ = 2048
COL_GK = 2560
COL_GV = 3072
COL_Z = 3584
COL_NQ = 4096
COL_KV6 = 4608
N_MAIN = 5376
SM_A = 0
SM_B = 4
SM_G = 8


def _cparams(*sem):
    return pltpu.CompilerParams(dimension_semantics=sem, vmem_limit_bytes=VMEM_LIMIT_BYTES)


def _resident(shape):
    nd = len(shape)
    return pl.BlockSpec(shape, lambda *_: (0,) * nd, pipeline_mode=pl.Buffered(1))


def _layer_norm(v, g, b):
    mu = jnp.mean(v, -1, keepdims=True)
    c = v - mu
    var = jnp.mean(c * c, -1, keepdims=True)
    return c * lax.rsqrt(var + LN_EPS) * g + b


def _silu(v):
    return v * (1.0 / (1.0 + jnp.exp(-v)))


def _sigmoid(v):
    return 1.0 / (1.0 + jnp.exp(-v))


def _dot(a, b):
    return jnp.dot(a, b, preferred_element_type=F32)


def _dot_nt(a, b):
    return lax.dot_general(a, b, (((1,), (1,)), ((), ())), preferred_element_type=F32)


def _proj_body(x_ref, w_ref, ws_ref, o_ref, os_ref, *, chunk):
    x = x_ref[...].astype(BF16)
    for c in range(0, w_ref.shape[1], chunk):
        o_ref[:, c:c + chunk] = _dot(x, w_ref[:, c:c + chunk]).astype(o_ref.dtype)
    os_ref[...] = _dot(x, ws_ref[...])


def _project(x2d, w_main, w_small, tm):
    t, d = x2d.shape
    n = w_main.shape[1]
    return pl.pallas_call(
        functools.partial(_proj_body, chunk=768),
        grid=(t // tm,),
        in_specs=[pl.BlockSpec((tm, d), lambda i: (i, 0)), _resident((d, n)), _resident((d, LANES))],
        out_specs=[pl.BlockSpec((tm, n), lambda i: (i, 0)), pl.BlockSpec((tm, LANES), lambda i: (i, 0))],
        out_shape=[jax.ShapeDtypeStruct((t, n), BF16), jax.ShapeDtypeStruct((t, LANES), F32)],
        compiler_params=_cparams("arbitrary"),
        name="proj",
    )(x2d, w_main, w_small)


def _rope_tables(pos_col, inv_row):
    ang = pos_col * inv_row
    cos = jnp.cos(ang)
    sin = jnp.sin(ang)
    lane = lax.broadcasted_iota(jnp.int32, ang.shape, 1)
    return cos, jnp.where(lane < HEAD_DIM // 2, -sin, sin)


def _rope(v, cos, sin_signed):
    return v * cos + pltpu.roll(v, HEAD_DIM // 2, 1) * sin_signed


def _nsa_prep_body(pos_ref, inv_ref, q_ref, ks_ref, kw_ref, qo_ref, kso_ref, kwo_ref, *, ct):
    i = pl.program_id(1)
    cos, sin_s = _rope_tables(pos_ref[0], inv_ref[...])
    for h in range(NSA_HEADS):
        sl = slice(h * HEAD_DIM, (h + 1) * HEAD_DIM)
        qo_ref[0, :, sl] = (_rope(q_ref[0, :, sl].astype(F32), cos, sin_s) * ATTN_SCALE).astype(BF16)
    kso_ref[0, :, :HEAD_DIM] = _rope(ks_ref[0].astype(F32), cos, sin_s).astype(BF16)
    lane = lax.broadcasted_iota(jnp.int32, (ct, LANES), 1)
    tok = lax.broadcasted_iota(jnp.int32, (ct, LANES), 0) + i * ct
    kso_ref[0, :, HEAD_DIM:] = jnp.where(lane == tok // SLC_BLOCK, 1.0, 0.0).astype(BF16)
    kwo_ref[0] = _rope(kw_ref[0].astype(F32), cos, sin_s).astype(BF16)


def _nsa_prep(proj3, posf, inv_row, ct):
    b, s, _ = proj3.shape
    cb = lambda col, w: col // w
    return pl.pallas_call(
        functools.partial(_nsa_prep_body, ct=ct),
        grid=(b, s // ct),
        in_specs=[
            pl.BlockSpec((1, ct, 1), lambda bi, i: (bi, i, 0)),
            _resident((1, LANES)),
            pl.BlockSpec((1, ct, NSA_Q), lambda bi, i: (bi, i, cb(COL_NQ, NSA_Q))),
            pl.BlockSpec((1, ct, HEAD_DIM), lambda bi, i: (bi, i, cb(COL_KV6, HEAD_DIM) + 2)),
            pl.BlockSpec((1, ct, HEAD_DIM), lambda bi, i: (bi, i, cb(COL_KV6, HEAD_DIM) + 4)),
        ],
        out_specs=[
            pl.BlockSpec((1, ct, NSA_Q), lambda bi, i: (bi, i, 0)),
            pl.BlockSpec((1, ct, 2 * HEAD_DIM), lambda bi, i: (bi, i, 0)),
            pl.BlockSpec((1, ct, HEAD_DIM), lambda bi, i: (bi, i, 0)),
        ],
        out_shape=[
            jax.ShapeDtypeStruct((b, s, NSA_Q), BF16),
            jax.ShapeDtypeStruct((b, s, 2 * HEAD_DIM), BF16),
            jax.ShapeDtypeStruct((b, s, HEAD_DIM), BF16),
        ],
        compiler_params=_cparams("arbitrary", "arbitrary"),
        name="nsa_prep",
    )(posf, inv_row, proj3, proj3, proj3)


def _compress_body(kc_ref, vc_ref, pk_ref, pv_ref, kw1_ref, kw2_ref, vw1_ref, vw2_ref, posc_ref, inv_ref,
                   ko_ref, vo_ref):
    def comp(seg_ref, p_ref, w1_ref, w2_ref):
        seg = seg_ref[0]
        n_seg, half = seg.shape
        first = _dot(seg, w1_ref[:half, :])
        second = _dot(seg, w1_ref[half:, :])
        pos_term = _dot(jnp.broadcast_to(p_ref[...], (8, p_ref.shape[1])).astype(BF16), w1_ref[...])[0:1]
        hidden = first + pltpu.roll(second, n_seg - 1, 0) + pos_term
        return _dot(_silu(hidden).astype(BF16), w2_ref[...])

    kc = comp(kc_ref, pk_ref, kw1_ref, kw2_ref)
    cos, sin_s = _rope_tables(posc_ref[0], inv_ref[...])
    ko_ref[0] = _rope(kc, cos, sin_s).astype(BF16)
    vo_ref[0] = comp(vc_ref, pv_ref, vw1_ref, vw2_ref).astype(BF16)


def _compress(kc_seg, vc_seg, pos_k, pos_v, kw1, kw2, vw1, vw2, posc, inv_row):
    b, n_seg, seg_w = kc_seg.shape
    seg_spec = pl.BlockSpec((1, n_seg, seg_w), lambda bi: (bi, 0, 0))
    out_spec = pl.BlockSpec((1, n_seg, HEAD_DIM), lambda bi: (bi, 0, 0))
    return pl.pallas_call(
        _compress_body,
        grid=(b,),
        in_specs=[seg_spec, seg_spec, _resident(pos_k.shape), _resident(pos_v.shape), _resident(kw1.shape),
                  _resident(kw2.shape), _resident(vw1.shape), _resident(vw2.shape),
                  pl.BlockSpec((1, n_seg, 1), lambda bi: (bi, 0, 0)), _resident((1, LANES))],
        out_specs=[out_spec, out_spec],
        out_shape=[jax.ShapeDtypeStruct((b, n_seg, HEAD_DIM), BF16)] * 2,
        compiler_params=_cparams("arbitrary"),
        name="compress",
    )(kc_seg, vc_seg, pos_k, pos_v, kw1, kw2, vw1, vw2, posc, inv_row)


def _split3(v):
    hi = v.astype(BF16)
    r1 = v - hi.astype(F32)
    mid = r1.astype(BF16)
    lo = (r1 - mid.astype(F32)).astype(BF16)
    return hi, mid, lo


def _nsa_attn_body(q_ref, sm_ref, kc_ref, vc_ref, ks_ref, vs_ref, kw_ref, vw_ref, o_ref,
                   m_sc, l_sc, acc_sc, *, tq, n_cmp_pad, n_slc, top_k, seq):
    qi = pl.program_id(1)
    t0 = qi * tq
    hq = NSA_HEADS * tq
    q = q_ref[0]
    q_st = jnp.concatenate([q[:, h * HEAD_DIM:(h + 1) * HEAD_DIM] for h in range(NSA_HEADS)], axis=0)

    s_c = _dot_nt(q_st, kc_ref[0])
    t_st = t0 + (lax.broadcasted_iota(jnp.int32, (hq, n_cmp_pad), 0) & (tq - 1))
    n_id = lax.broadcasted_iota(jnp.int32, (hq, n_cmp_pad), 1)
    valid_c = (n_id * CMP_STRIDE + (CMP_LEN - 1)) <= t_st
    m_c = jnp.max(jnp.where(valid_c, s_c, NEG), -1, keepdims=True)
    e_c = jnp.where(valid_c, jnp.exp(s_c - m_c), 0.0)
    l_c = jnp.sum(e_c, -1, keepdims=True)
    p_c = e_c * (1.0 / jnp.where(l_c > 0.0, l_c, 1.0))
    o_c = _dot(p_c.astype(BF16), vc_ref[0])

    p_sum = p_c[0:tq]
    for h in range(1, NSA_HEADS):
        p_sum = p_sum + p_c[h * tq:(h + 1) * tq]
    on = lax.broadcasted_iota(jnp.int32, (n_cmp_pad, LANES), 0) * CMP_STRIDE
    oj = lax.broadcasted_iota(jnp.int32, (n_cmp_pad, LANES), 1) * SLC_BLOCK
    overlap = jnp.where((on < oj + SLC_BLOCK) & (on + CMP_LEN > oj), 1.0, 0.0).astype(BF16)
    imp = sum(_dot(part, overlap) for part in _split3(p_sum))

    j = lax.broadcasted_iota(jnp.int32, (tq, LANES), 1)
    t_q = t0 + lax.broadcasted_iota(jnp.int32, (tq, LANES), 0)
    cur = t_q // SLC_BLOCK
    forced = (j == 0) | (j == cur) | (j == cur - 1)
    cand = jnp.where(jnp.logical_not(forced) & (j <= cur) & (j < n_slc), imp, -1.0)
    sel = jnp.where(forced, 1.0, 0.0)
    for _ in range(max(top_k - 3, 0)):
        best = jnp.max(cand, -1, keepdims=True)
        idx = jnp.min(jnp.where(cand == best, j, LANES), -1, keepdims=True)
        hit = j == idx
        sel = jnp.where(hit & (best >= 0.0), 1.0, sel)
        cand = jnp.where(hit, -2.0, cand)
    bias = jnp.where(sel > 0.0, 0.0, NEG).astype(BF16)

    q_aug = jnp.concatenate(
        [jnp.concatenate([q[:, h * HEAD_DIM:(h + 1) * HEAD_DIM], bias], axis=1) for h in range(NSA_HEADS)], axis=0)

    m_sc[...] = jnp.full((hq, 1), NEG, F32)
    l_sc[...] = jnp.zeros((hq, 1), F32)
    acc_sc[...] = jnp.zeros((hq, HEAD_DIM), F32)

    def flash_step(k_tile, v_tile, mask):
        s = _dot_nt(q_aug, k_tile)
        if mask is not None:
            s = jnp.where(mask, s, NEG)
        m_old = m_sc[...]
        m_new = jnp.maximum(m_old, jnp.max(s, -1, keepdims=True))
        alpha = jnp.exp(m_old - m_new)
        p = jnp.exp(s - m_new)
        l_sc[...] = alpha * l_sc[...] + jnp.sum(p, -1, keepdims=True)
        acc_sc[...] = alpha * acc_sc[...] + _dot(p.astype(BF16), v_tile)
        m_sc[...] = m_new

    def full_tile(kb, carry):
        off = pl.multiple_of(kb * tq, tq)
        flash_step(ks_ref[0, pl.ds(off, tq), :], vs_ref[0, pl.ds(off, tq), :], None)
        return carry

    lax.fori_loop(0, qi, full_tile, 0)
    off_d = pl.multiple_of(t0, tq)
    ql = lax.broadcasted_iota(jnp.int32, (hq, tq), 0) & (tq - 1)
    kl = lax.broadcasted_iota(jnp.int32, (hq, tq), 1)
    flash_step(ks_ref[0, pl.ds(off_d, tq), :], vs_ref[0, pl.ds(off_d, tq), :], kl <= ql)
    o_s = acc_sc[...] * (1.0 / l_sc[...])

    wk = tq + WINDOW
    start = pl.multiple_of(jnp.maximum(t0 - WINDOW, 0), tq)
    s_w = _dot_nt(q_st, kw_ref[0, pl.ds(start, wk), :])
    kpos = start + lax.broadcasted_iota(jnp.int32, (hq, wk), 1)
    t_w = t0 + (lax.broadcasted_iota(jnp.int32, (hq, wk), 0) & (tq - 1))
    rel = t_w - kpos
    s_w = jnp.where((rel >= 0) & (rel < WINDOW), s_w, NEG)
    m_w = jnp.max(s_w, -1, keepdims=True)
    e_w = jnp.exp(s_w - m_w)
    p_w = e_w * (1.0 / jnp.sum(e_w, -1, keepdims=True))
    o_w = _dot(p_w.astype(BF16), vw_ref[0, pl.ds(start, wk), :])

    gates = _sigmoid(sm_ref[0])
    for h in range(NSA_HEADS):
        rows = slice(h * tq, (h + 1) * tq)
        c0 = SM_G + 3 * h
        o = (gates[:, c0:c0 + 1] * o_c[rows] + gates[:, c0 + 1:c0 + 2] * o_s[rows]
             + gates[:, c0 + 2:c0 + 3] * o_w[rows])
        o_ref[0, :, h * HEAD_DIM:(h + 1) * HEAD_DIM] = o.astype(o_ref.dtype)


def _nsa_attention(q_rope, small3, k_cmp, v_cmp, ks_aug, proj3, kw_rope, tq):
    b, s, _ = q_rope.shape
    n_cmp_pad = k_cmp.shape[1]
    n_slc = s // SLC_BLOCK
    top_k = min(SLC_TOPK, n_slc)
    kv_col = COL_KV6 // HEAD_DIM
    hq = NSA_HEADS * tq
    batch_spec = lambda shape: pl.BlockSpec((1,) + shape, lambda bi, i: (bi, 0, 0))
    return pl.pallas_call(
        functools.partial(_nsa_attn_body, tq=tq, n_cmp_pad=n_cmp_pad, n_slc=n_slc, top_k=top_k, seq=s),
        grid=(b, s // tq),
        in_specs=[
            pl.BlockSpec((1, tq, NSA_Q), lambda bi, i: (bi, i, 0)),
            pl.BlockSpec((1, tq, LANES), lambda bi, i: (bi, i, 0)),
            batch_spec((n_cmp_pad, HEAD_DIM)),
            batch_spec((n_cmp_pad, HEAD_DIM)),
            batch_spec((s, 2 * HEAD_DIM)),
            pl.BlockSpec((1, s, HEAD_DIM), lambda bi, i: (bi, 0, kv_col + 3)),
            batch_spec((s, HEAD_DIM)),
            pl.BlockSpec((1, s, HEAD_DIM), lambda bi, i: (bi, 0, kv_col + 5)),
        ],
        out_specs=pl.BlockSpec((1, tq, NSA_Q), lambda bi, i: (bi, i, 0)),
        out_shape=jax.ShapeDtypeStruct((b, s, NSA_Q), BF16),
        scratch_shapes=[pltpu.VMEM((hq, 1), F32), pltpu.VMEM((hq, 1), F32), pltpu.VMEM((hq, HEAD_DIM), F32)],
        compiler_params=_cparams("arbitrary", "arbitrary"),
        name="nsa_attn",
    )(q_rope, small3, k_cmp, v_cmp, ks_aug, proj3, kw_rope, proj3)


def _gdn_body(xq_ref, xk_ref, xv_ref, z_ref, sm_ref, cw_ref, hp_ref, nw_ref, o_ref,
              tail_sc, state_sc, qd_sc, kt_sc, u_sc, w_sc, at_sc, eg_sc, *, ct):
    i = pl.program_id(1)
    c = GDN_CHUNK
    nc = ct // c

    @pl.when(i == 0)
    def _():
        tail_sc[...] = jnp.zeros_like(tail_sc)
        state_sc[...] = jnp.zeros_like(state_sc)

    row = lax.broadcasted_iota(jnp.int32, (ct, LANES), 0)
    row_in_chunk = row & (c - 1)
    ii = lax.broadcasted_iota(jnp.int32, (nc, c, c), 1)
    jj = lax.broadcasted_iota(jnp.int32, (nc, c, c), 2)

    def conv_silu(x_ref, part, h):
        col = slice(h * HEAD_DIM, (h + 1) * HEAD_DIM)
        wcol = slice(part * GDN_QK + h * HEAD_DIM, part * GDN_QK + (h + 1) * HEAD_DIM)
        x = x_ref[0, :, col].astype(F32)
        xcat = jnp.concatenate([tail_sc[part, :, col], x], axis=0)
        acc = xcat[8:] * cw_ref[GDN_CONV - 1:GDN_CONV, wcol]
        for tap in range(GDN_CONV - 1):
            shift = GDN_CONV - 1 - tap
            acc = acc + pltpu.roll(xcat, shift, 0)[8:] * cw_ref[tap:tap + 1, wcol]
        tail_sc[part, :, col] = x[ct - 8:]
        return _silu(acc)

    def l2n(v):
        return v * lax.rsqrt(jnp.sum(v * v, -1, keepdims=True) + RMS_EPS)

    sm = sm_ref[0]
    for h in range(GDN_HEADS):
        q = l2n(conv_silu(xq_ref, 0, h)) * (HEAD_DIM ** -0.5)
        k = l2n(conv_silu(xk_ref, 1, h))
        v = conv_silu(xv_ref, 2, h)
        a = sm[:, SM_A + h:SM_A + h + 1]
        bcol = sm[:, SM_B + h:SM_B + h + 1]
        a_log = hp_ref[0:1, h:h + 1]
        dt_bias = hp_ref[1:2, h:h + 1]
        sp_in = a + dt_bias
        softplus = jnp.maximum(sp_in, 0.0) + jnp.log(1.0 + jnp.exp(-jnp.abs(sp_in)))
        g = -jnp.exp(a_log) * softplus
        beta = _sigmoid(bcol)
        gc = jnp.broadcast_to(g, (ct, LANES))
        step = 1
        while step < c:
            gc = gc + jnp.where(row_in_chunk >= step, pltpu.roll(gc, step, 0), 0.0)
            step *= 2
        egc = jnp.exp(gc)
        kb = k * beta
        vb = v * beta
        gc3 = gc.reshape(nc, c, LANES)
        g_col = gc3[:, :, :c]
        g_row = jnp.swapaxes(gc3, 1, 2)[:, :c, :]
        decay = jnp.where(ii >= jj, jnp.exp(jnp.minimum(g_col - g_row, 0.0)), 0.0)
        k3 = k.reshape(nc, c, LANES).astype(BF16)
        kk = jnp.einsum("nid,njd->nij", kb.reshape(nc, c, LANES).astype(BF16), k3, preferred_element_type=F32)
        a_mat = jnp.where(ii > jj, kk * decay, 0.0)
        n_mat = -a_mat
        pw = a_mat
        sq = 2
        while sq < c:
            pwb = pw.astype(BF16)
            pw = jnp.einsum("nij,njk->nik", pwb, pwb, preferred_element_type=F32)
            n_mat = n_mat + pw + jnp.einsum("nij,njk->nik", n_mat.astype(BF16), pw.astype(BF16),
                                            preferred_element_type=F32)
            sq *= 2
        nb = n_mat.astype(BF16)
        vb3 = vb.reshape(nc, c, LANES)
        kbg3 = (kb * egc).reshape(nc, c, LANES)
        u = vb3 + jnp.einsum("nij,njd->nid", nb, vb3.astype(BF16), preferred_element_type=F32)
        w = kbg3 + jnp.einsum("nij,njd->nid", nb, kbg3.astype(BF16), preferred_element_type=F32)
        attn = jnp.einsum("nid,njd->nij", q.reshape(nc, c, LANES).astype(BF16), k3,
                          preferred_element_type=F32) * decay
        g_last = jnp.broadcast_to(gc3[:, c - 1:c, :], (nc, c, LANES))
        qd_sc[h] = (q * egc).astype(BF16)
        kt_sc[h] = (k.reshape(nc, c, LANES) * jnp.exp(g_last - gc3)).reshape(ct, LANES)
        u_sc[h] = u.reshape(ct, LANES)
        w_sc[h] = w.reshape(ct, LANES).astype(BF16)
        at_sc[h] = attn.reshape(ct, c).astype(BF16)
        eg_sc[h] = jnp.exp(g_last).reshape(ct, LANES)

    nw = nw_ref[...]

    def chunk_step(n, carry):
        off = pl.multiple_of(n * c, c)
        rows = pl.ds(off, c)
        for h in range(GDN_HEADS):
            state = state_sc[h]
            sb = state.astype(BF16)
            v_new = u_sc[h, rows, :] - _dot(w_sc[h, rows, :], sb)
            vnb = v_new.astype(BF16)
            o = _dot(qd_sc[h, rows, :], sb) + _dot(at_sc[h, rows, :], vnb)
            decay_last = eg_sc[h, pl.ds(off, 1), :]
            kt_t = kt_sc[h, rows, :].T.astype(BF16)
            state_sc[h] = state * decay_last + _dot(kt_t, vnb)
            zf = z_ref[0, rows, h * HEAD_DIM:(h + 1) * HEAD_DIM].astype(F32)
            o = o * lax.rsqrt(jnp.mean(o * o, -1, keepdims=True) + RMS_EPS) * nw * _silu(zf)
            o_ref[0, rows, h * HEAD_DIM:(h + 1) * HEAD_DIM] = o.astype(o_ref.dtype)
        return carry

    lax.fori_loop(0, nc, chunk_step, 0)


def _gdn(proj3, small3, conv_w, hp, norm_w, ct):
    b, s, _ = proj3.shape
    cb = lambda col: col // GDN_QK
    blk = lambda col: pl.BlockSpec((1, ct, GDN_QK), lambda bi, i: (bi, i, cb(col)))
    hsh = (GDN_HEADS, ct, LANES)
    return pl.pallas_call(
        functools.partial(_gdn_body, ct=ct),
        grid=(b, s // ct),
        in_specs=[blk(COL_GQ), blk(COL_GK), blk(COL_GV), blk(COL_Z),
                  pl.BlockSpec((1, ct, LANES), lambda bi, i: (bi, i, 0)),
                  _resident(conv_w.shape), _resident(hp.shape), _resident(norm_w.shape)],
        out_specs=pl.BlockSpec((1, ct, GDN_V), lambda bi, i: (bi, i, 0)),
        out_shape=jax.ShapeDtypeStruct((b, s, GDN_V), BF16),
        scratch_shapes=[
            pltpu.VMEM((3, 8, GDN_QK), F32),
            pltpu.VMEM((GDN_HEADS, HEAD_DIM, HEAD_DIM), F32),
            pltpu.VMEM(hsh, BF16),
            pltpu.VMEM(hsh, F32),
            pltpu.VMEM(hsh, F32),
            pltpu.VMEM(hsh, BF16),
            pltpu.VMEM((GDN_HEADS, ct, GDN_CHUNK), BF16),
            pltpu.VMEM(hsh, F32),
        ],
        compiler_params=_cparams("arbitrary", "arbitrary"),
        name="gdn",
    )(proj3, proj3, proj3, proj3, small3, conv_w, hp, norm_w)


def _merge_body(x_ref, oa_ref, ob_ref, ga_ref, gb_ref, wa_ref, wb_ref, wm_ref, g_ref, b_ref, o_ref):
    ya = _dot(oa_ref[...], wa_ref[...])
    yb = _dot(ob_ref[...], wb_ref[...])
    y = _sigmoid(ga_ref[...].astype(F32)) * ya + _sigmoid(gb_ref[...].astype(F32)) * yb
    out = _dot(y.astype(BF16), wm_ref[...])
    o_ref[...] = _layer_norm(DN_ALPHA * x_ref[...] + out, g_ref[...], b_ref[...])


def _merge(x2d, oa, ob, proj2d, wa, wb, wm, g, bias, tm):
    t, d = x2d.shape
    row = lambda w, cbi=0: pl.BlockSpec((tm, w), lambda i: (i, cbi))
    return pl.pallas_call(
        _merge_body,
        grid=(t // tm,),
        in_specs=[row(d), row(GDN_V), row(NSA_Q), row(d, COL_GATE_A // d), row(d, COL_GATE_B // d),
                  _resident(wa.shape), _resident(wb.shape), _resident(wm.shape), _resident(g.shape),
                  _resident(bias.shape)],
        out_specs=row(d),
        out_shape=jax.ShapeDtypeStruct((t, d), F32),
        compiler_params=_cparams("arbitrary"),
        name="merge",
    )(x2d, oa, ob, proj2d, proj2d, wa, wb, wm, g, bias)


def _memkv_body(m_ref, wk_ref, wv_ref, k_ref, v_ref):
    m = m_ref[...].astype(BF16)
    k_ref[...] = _dot(m, wk_ref[...]).astype(BF16)
    v_ref[...] = _dot(m, wv_ref[...]).astype(BF16)


def _memkv(mem2d, wk, wv, tm):
    t, d = mem2d.shape
    row = pl.BlockSpec((tm, d), lambda i: (i, 0))
    return pl.pallas_call(
        _memkv_body,
        grid=(t // tm,),
        in_specs=[row, _resident(wk.shape), _resident(wv.shape)],
        out_specs=[row, row],
        out_shape=[jax.ShapeDtypeStruct((t, d), BF16)] * 2,
        compiler_params=_cparams("arbitrary"),
        name="memkv",
    )(mem2d, wk, wv)


def _xattn_body(x_ref, k_ref, v_ref, wq_ref, wo_ref, g_ref, b_ref, o_ref):
    x = x_ref[0]
    q = _dot(x.astype(BF16), wq_ref[...]).astype(BF16)
    heads = []
    for h in range(XA_HEADS):
        sl = slice(h * XA_HEAD_DIM, (h + 1) * XA_HEAD_DIM)
        s = _dot_nt(q[:, sl], k_ref[0, :, sl]) * (XA_HEAD_DIM ** -0.5)
        m = jnp.max(s, -1, keepdims=True)
        e = jnp.exp(s - m)
        p = e * (1.0 / jnp.sum(e, -1, keepdims=True))
        heads.append(_dot(p.astype(BF16), v_ref[0, :, sl]).astype(BF16))
    o = jnp.concatenate(heads, axis=1)
    out = _dot(o, wo_ref[...])
    o_ref[0] = _layer_norm(DN_ALPHA * x + out, g_ref[...], b_ref[...])


def _xattn(x3, k3, v3, wq, wo, g, bias, tm):
    b, s, d = x3.shape
    m = k3.shape[1]
    return pl.pallas_call(
        _xattn_body,
        grid=(b, s // tm),
        in_specs=[pl.BlockSpec((1, tm, d), lambda bi, i: (bi, i, 0)),
                  pl.BlockSpec((1, m, d), lambda bi, i: (bi, 0, 0)),
                  pl.BlockSpec((1, m, d), lambda bi, i: (bi, 0, 0)),
                  _resident(wq.shape), _resident(wo.shape), _resident(g.shape), _resident(bias.shape)],
        out_specs=pl.BlockSpec((1, tm, d), lambda bi, i: (bi, i, 0)),
        out_shape=jax.ShapeDtypeStruct((b, s, d), F32),
        compiler_params=_cparams("arbitrary", "arbitrary"),
        name="xattn",
    )(x3, k3, v3, wq, wo, g, bias)


def _ffn_body(x_ref, wg_ref, wu_ref, wd_ref, g_ref, b_ref, o_ref, *, chunk):
    x = x_ref[...]
    xb = x.astype(BF16)
    out = None
    for c in range(0, wg_ref.shape[1], chunk):
        hid = _silu(_dot(xb, wg_ref[:, c:c + chunk])) * _dot(xb, wu_ref[:, c:c + chunk])
        part = _dot(hid.astype(BF16), wd_ref[c:c + chunk, :])
        out = part if out is None else out + part
    o_ref[...] = _layer_norm(DN_ALPHA * x + out, g_ref[...], b_ref[...])


def _ffn(x2d, wg, wu, wd, g, bias, tm):
    t, d = x2d.shape
    dff = wg.shape[1]
    chunk = dff // 2 if (dff // 2) % LANES == 0 else dff
    row = pl.BlockSpec((tm, d), lambda i: (i, 0))
    return pl.pallas_call(
        functools.partial(_ffn_body, chunk=chunk),
        grid=(t // tm,),
        in_specs=[row, _resident(wg.shape), _resident(wu.shape), _resident(wd.shape), _resident(g.shape),
                  _resident(bias.shape)],
        out_specs=row,
        out_shape=jax.ShapeDtypeStruct((t, d), F32),
        compiler_params=_cparams("arbitrary"),
        name="ffn",
    )(x2d, wg, wu, wd, g, bias)


def _tile(n, pref):
    t = min(n, pref)
    assert n % t == 0, (n, pref)
    return t


def _layer(x, mem, positions, w_in, conv_w, a_log, dt_bias, gdn_norm_w, cmp_pos_k, cmp_pos_v, cmp_k_w1, cmp_k_w2,
           cmp_v_w1, cmp_v_w2, w_up_gdn, w_up_nsa, w_mix_out, ln1_g, ln1_b, xa_wq, xa_wk, xa_wv, xa_wo, ln2_g,
           ln2_b, ffn_w_gate, ffn_w_up, ffn_w_down, ln3_g, ln3_b):
    b, s, d = x.shape
    t = b * s
    assert d == D_MODEL and s % WINDOW == 0 and s // SLC_BLOCK <= LANES and s >= 2 * WINDOW

    o_qkv, o_z = 0, 3 * GDN_QK
    o_a = o_z + GDN_V
    o_b = o_a + GDN_HEADS
    o_qn = o_b + GDN_HEADS
    o_kv6 = o_qn + NSA_Q
    o_gn = o_kv6 + 6 * HEAD_DIM
    o_ga = o_gn + 3 * NSA_HEADS
    o_gb = o_ga + D_MODEL
    w_main = jnp.concatenate([w_in[:, o_ga:o_gb], w_in[:, o_gb:o_gb + D_MODEL], w_in[:, o_qkv:o_z], w_in[:, o_z:o_a],
                              w_in[:, o_qn:o_kv6], w_in[:, o_kv6:o_gn]], axis=1).astype(BF16)
    n_small = 2 * GDN_HEADS + 3 * NSA_HEADS
    w_small = jnp.concatenate([w_in[:, o_a:o_qn], w_in[:, o_gn:o_ga], jnp.zeros((d, LANES - n_small), w_in.dtype)],
                              axis=1).astype(BF16)

    x2d = x.reshape(t, d)
    proj2d, small2d = _project(x2d, w_main, w_small, _tile(t, 512))
    proj3 = proj2d.reshape(b, s, N_MAIN)
    small3 = small2d.reshape(b, s, LANES)

    hp = jnp.zeros((8, LANES), F32).at[0, :GDN_HEADS].set(a_log.astype(F32)).at[1, :GDN_HEADS].set(
        dt_bias.astype(F32))
    o_gdn = _gdn(proj3, small3, conv_w.astype(F32), hp, gdn_norm_w.reshape(1, HEAD_DIM).astype(F32), _tile(s, 512))

    half = HEAD_DIM // 2
    inv = ROPE_THETA ** (-jnp.arange(half, dtype=F32) / half)
    inv_row = jnp.concatenate([inv, inv]).reshape(1, HEAD_DIM)
    posf = positions.astype(F32)
    q_rope, ks_aug, kw_rope = _nsa_prep(proj3, posf.reshape(b, s, 1), inv_row, _tile(s, 512))

    n_seg = s // CMP_STRIDE
    seg_w = CMP_STRIDE * HEAD_DIM
    kc_seg = proj3[:, :, COL_KV6:COL_KV6 + HEAD_DIM].reshape(b, n_seg, seg_w)
    vc_seg = proj3[:, :, COL_KV6 + HEAD_DIM:COL_KV6 + 2 * HEAD_DIM].reshape(b, n_seg, seg_w)
    posc = posf[:, CMP_LEN - 1::CMP_STRIDE]
    posc = jnp.concatenate([posc, posc[:, -1:]], axis=1).reshape(b, n_seg, 1)
    k_cmp, v_cmp = _compress(kc_seg, vc_seg, cmp_pos_k.reshape(1, CMP_LEN * HEAD_DIM).astype(F32),
                             cmp_pos_v.reshape(1, CMP_LEN * HEAD_DIM).astype(F32), cmp_k_w1.astype(BF16),
                             cmp_k_w2.astype(BF16), cmp_v_w1.astype(BF16), cmp_v_w2.astype(BF16), posc, inv_row)
    o_nsa = _nsa_attention(q_rope, small3, k_cmp, v_cmp, ks_aug, proj3, kw_rope, _tile(s, 256))

    vec = lambda p: p.reshape(1, d).astype(F32)
    x1 = _merge(x2d, o_gdn.reshape(t, GDN_V), o_nsa.reshape(t, NSA_Q), proj2d, w_up_gdn.astype(BF16),
                w_up_nsa.astype(BF16), w_mix_out.astype(BF16), vec(ln1_g), vec(ln1_b), _tile(t, 512))
    mlen = mem.shape[1]
    mk, mv = _memkv(mem.reshape(b * mlen, d), xa_wk.astype(BF16), xa_wv.astype(BF16), _tile(b * mlen, 512))
    x2 = _xattn(x1.reshape(b, s, d), mk.reshape(b, mlen, d), mv.reshape(b, mlen, d), xa_wq.astype(BF16),
                xa_wo.astype(BF16), vec(ln2_g), vec(ln2_b), _tile(s, 512))
    x3 = _ffn(x2.reshape(t, d), ffn_w_gate.astype(BF16), ffn_w_up.astype(BF16), ffn_w_down.astype(BF16),
              vec(ln3_g), vec(ln3_b), _tile(t, 512))
    return x3.reshape(b, s, d)


def kernel(x, mem, positions, w_in, gdn_conv_w, gdn_a_log, gdn_dt_bias, gdn_norm_w, cmp_pos_k, cmp_pos_v, cmp_k_w1, cmp_k_w2, cmp_v_w1, cmp_v_w2, w_up_gdn, w_up_nsa, w_mix_out, ln1_g, ln1_b, xa_wq, xa_wk, xa_wv, xa_wo, ln2_g, ln2_b, ffn_w_gate, ffn_w_up, ffn_w_down, ln3_g, ln3_b):
    for l in range(w_in.shape[0]):
        x = _layer(x, mem, positions, w_in[l], gdn_conv_w[l], gdn_a_log[l], gdn_dt_bias[l], gdn_norm_w[l],
                   cmp_pos_k[l], cmp_pos_v[l], cmp_k_w1[l], cmp_k_w2[l], cmp_v_w1[l], cmp_v_w2[l], w_up_gdn[l],
                   w_up_nsa[l], w_mix_out[l], ln1_g[l], ln1_b[l], xa_wq[l], xa_wk[l], xa_wv[l], xa_wo[l], ln2_g[l],
                   ln2_b[l], ffn_w_gate[l], ffn_w_up[l], ffn_w_down[l], ln3_g[l], ln3_b[l])
    return x
```

```python
import functools

import jax
import jax.numpy as jnp
from jax import lax
from jax.experimental import pallas as pl
from jax.experimental.pallas import tpu as pltpu

F32 = jnp.float32
BF16 = jnp.bfloat16

D_MODEL = 1024
HEAD_DIM = 128
GDN_HEADS = 4
GDN_QK = GDN_HEADS * HEAD_DIM
GDN_V = GDN_HEADS * HEAD_DIM
GDN_CONV = 4
GDN_CHUNK = 64
NSA_HEADS = 4
NSA_Q = NSA_HEADS * HEAD_DIM
CMP_LEN = 32
CMP_STRIDE = 16
SLC_BLOCK = 64
SLC_TOPK = 16
WINDOW = 512
FORCE_SCORE = 1.0e4
XA_HEADS = 4
XA_HEAD_DIM = D_MODEL // XA_HEADS
ROPE_THETA = 10000.0
LN_EPS = 1e-5
RMS_EPS = 1e-6
NEG = -1.0e30
ATTN_SCALE = HEAD_DIM ** -0.5
LOG2E = 1.4426950408889634
DEPTH = 1
DN_ALPHA = (2 * DEPTH) ** 0.25

LANES = 128
VMEM_LIMIT_BYTES = 56 * 1024 * 1024

COL_GATE_A = 0
COL_GATE_B = 1024
COL_GQ = 2048
COL_GK = 2560
COL_GV = 3072
COL_Z = 3584
COL_NQ = 4096
COL_KV6 = 4608
N_MAIN = 5376
SM_A = 0
SM_B = 4
SM_G = 8


def _cparams(*sem):
    return pltpu.CompilerParams(dimension_semantics=sem, vmem_limit_bytes=VMEM_LIMIT_BYTES)


def _resident(shape):
    nd = len(shape)
    return pl.BlockSpec(shape, lambda *_: (0,) * nd, pipeline_mode=pl.Buffered(1))


def _layer_norm(v, g, b):
    mu = jnp.mean(v, -1, keepdims=True)
    c = v - mu
    var = jnp.mean(c * c, -1, keepdims=True)
    return c * lax.rsqrt(var + LN_EPS) * g + b


def _silu(v):
    return v * (1.0 / (1.0 + jnp.exp(-v)))


def _sigmoid(v):
    return 1.0 / (1.0 + jnp.exp(-v))


def _dot(a, b):
    return jnp.dot(a, b, preferred_element_type=F32)


def _dot_nt(a, b):
    return lax.dot_general(a, b, (((1,), (1,)), ((), ())), preferred_element_type=F32)


def _proj_body(x_ref, w_ref, ws_ref, o_ref, os_ref, *, chunk):
    x = x_ref[...].astype(BF16)
    for c in range(0, w_ref.shape[1], chunk):
        o_ref[:, c:c + chunk] = _dot(x, w_ref[:, c:c + chunk]).astype(o_ref.dtype)
    os_ref[...] = _dot(x, ws_ref[...])


def _project(x2d, w_main, w_small, tm):
    t, d = x2d.shape
    n = w_main.shape[1]
    return pl.pallas_call(
        functools.partial(_proj_body, chunk=768),
        grid=(t // tm,),
        in_specs=[pl.BlockSpec((tm, d), lambda i: (i, 0)), _resident((d, n)), _resident((d, LANES))],
        out_specs=[pl.BlockSpec((tm, n), lambda i: (i, 0)), pl.BlockSpec((tm, LANES), lambda i: (i, 0))],
        out_shape=[jax.ShapeDtypeStruct((t, n), BF16), jax.ShapeDtypeStruct((t, LANES), F32)],
        compiler_params=_cparams("arbitrary"),
        name="proj",
    )(x2d, w_main, w_small)


def _rope_tables(pos_col, inv_row):
    ang = pos_col * inv_row
    cos = jnp.cos(ang)
    sin = jnp.sin(ang)
    lane = lax.broadcasted_iota(jnp.int32, ang.shape, 1)
    return cos, jnp.where(lane < HEAD_DIM // 2, -sin, sin)


def _rope(v, cos, sin_signed):
    return v * cos + pltpu.roll(v, HEAD_DIM // 2, 1) * sin_signed


def _nsa_prep_body(pos_ref, inv_ref, q_ref, ks_ref, vs_ref, kw_ref, vw_ref,
                   qt_ref, kso_ref, vst_ref, kwo_ref, vwt_ref, *, ct, tk):
    i = pl.program_id(1)
    cos, sin_s = _rope_tables(pos_ref[0], inv_ref[...])
    for h in range(NSA_HEADS):
        sl = slice(h * HEAD_DIM, (h + 1) * HEAD_DIM)
        q_rot = _rope(q_ref[0, :, sl].astype(F32), cos, sin_s) * (ATTN_SCALE * LOG2E)
        qt_ref[0, sl, :] = q_rot.T.astype(BF16)
    kso_ref[0, :, :HEAD_DIM] = _rope(ks_ref[0].astype(F32), cos, sin_s).astype(BF16)
    lane = lax.broadcasted_iota(jnp.int32, (ct, LANES), 1)
    tok = lax.broadcasted_iota(jnp.int32, (ct, LANES), 0) + i * ct
    kso_ref[0, :, HEAD_DIM:] = jnp.where(lane == tok // SLC_BLOCK, 1.0, 0.0).astype(BF16)
    kwo_ref[0] = _rope(kw_ref[0].astype(F32), cos, sin_s).astype(BF16)
    for v_ref, vt_ref in ((vs_ref, vst_ref), (vw_ref, vwt_ref)):
        v_t = v_ref[0].astype(F32).T
        for n in range(ct // tk):
            vt_ref[0, n] = v_t[:, n * tk:(n + 1) * tk].astype(BF16)


def _nsa_prep(proj3, posf, inv_row, ct, tk):
    b, s, _ = proj3.shape
    kv_col = COL_KV6 // HEAD_DIM
    col = lambda c: pl.BlockSpec((1, ct, HEAD_DIM), lambda bi, i: (bi, i, c))
    vt_spec = pl.BlockSpec((1, ct // tk, HEAD_DIM, tk), lambda bi, i: (bi, i, 0, 0))
    vt_shape = jax.ShapeDtypeStruct((b, s // tk, HEAD_DIM, tk), BF16)
    return pl.pallas_call(
        functools.partial(_nsa_prep_body, ct=ct, tk=tk),
        grid=(b, s // ct),
        in_specs=[
            pl.BlockSpec((1, ct, 1), lambda bi, i: (bi, i, 0)),
            _resident((1, LANES)),
            pl.BlockSpec((1, ct, NSA_Q), lambda bi, i: (bi, i, COL_NQ // NSA_Q)),
            col(kv_col + 2), col(kv_col + 3), col(kv_col + 4), col(kv_col + 5),
        ],
        out_specs=[
            pl.BlockSpec((1, NSA_Q, ct), lambda bi, i: (bi, 0, i)),
            pl.BlockSpec((1, ct, 2 * HEAD_DIM), lambda bi, i: (bi, i, 0)),
            vt_spec,
            pl.BlockSpec((1, ct, HEAD_DIM), lambda bi, i: (bi, i, 0)),
            vt_spec,
        ],
        out_shape=[
            jax.ShapeDtypeStruct((b, NSA_Q, s), BF16),
            jax.ShapeDtypeStruct((b, s, 2 * HEAD_DIM), BF16),
            vt_shape,
            jax.ShapeDtypeStruct((b, s, HEAD_DIM), BF16),
            vt_shape,
        ],
        compiler_params=_cparams("arbitrary", "arbitrary"),
        name="nsa_prep",
    )(posf, inv_row, proj3, proj3, proj3, proj3, proj3)


def _compress_body(kc_ref, vc_ref, pk_ref, pv_ref, kw1_ref, kw2_ref, vw1_ref, vw2_ref, posc_ref, inv_ref,
                   ko_ref, vo_ref):
    def comp(seg_ref, p_ref, w1_ref, w2_ref, transposed):
        seg = seg_ref[0]
        n_seg, half = seg.shape
        first = _dot(seg, w1_ref[:half, :])
        second = _dot(seg, w1_ref[half:, :])
        pos_term = _dot(jnp.broadcast_to(p_ref[...], (8, p_ref.shape[1])).astype(BF16), w1_ref[...])[0:1]
        hidden = first + pltpu.roll(second, n_seg - 1, 0) + pos_term
        act = _silu(hidden)
        if transposed:
            return _dot(w2_ref[...], act.T.astype(BF16))
        return _dot(act.astype(BF16), w2_ref[...])

    kc = comp(kc_ref, pk_ref, kw1_ref, kw2_ref, False)
    cos, sin_s = _rope_tables(posc_ref[0], inv_ref[...])
    ko_ref[0] = _rope(kc, cos, sin_s).astype(BF16)
    vo_ref[0] = comp(vc_ref, pv_ref, vw1_ref, vw2_ref, True).astype(BF16)


def _compress(kc_seg, vc_seg, pos_k, pos_v, kw1, kw2, vw1, vw2, posc, inv_row):
    b, n_seg, seg_w = kc_seg.shape
    seg_spec = pl.BlockSpec((1, n_seg, seg_w), lambda bi: (bi, 0, 0))
    return pl.pallas_call(
        _compress_body,
        grid=(b,),
        in_specs=[seg_spec, seg_spec, _resident(pos_k.shape), _resident(pos_v.shape), _resident(kw1.shape),
                  _resident(kw2.shape), _resident(vw1.shape), _resident(vw2.shape),
                  pl.BlockSpec((1, n_seg, 1), lambda bi: (bi, 0, 0)), _resident((1, LANES))],
        out_specs=[pl.BlockSpec((1, n_seg, HEAD_DIM), lambda bi: (bi, 0, 0)),
                   pl.BlockSpec((1, HEAD_DIM, n_seg), lambda bi: (bi, 0, 0))],
        out_shape=[jax.ShapeDtypeStruct((b, n_seg, HEAD_DIM), BF16),
                   jax.ShapeDtypeStruct((b, HEAD_DIM, n_seg), BF16)],
        compiler_params=_cparams("arbitrary"),
        name="compress",
    )(kc_seg, vc_seg, pos_k, pos_v, kw1, kw2, vw1, vw2, posc, inv_row)


def _split3(v):
    hi = v.astype(BF16)
    r1 = v - hi.astype(F32)
    mid = r1.astype(BF16)
    lo = (r1 - mid.astype(F32)).astype(BF16)
    return hi, mid, lo


def _nsa_attn_body(qt_ref, sm_ref, kc_ref, vct_ref, ks_ref, vst_ref, kw_ref, vwt_ref, o_ref,
                   m_sc, l_sc, acc_sc, m2_sc, l2_sc, acc2_sc, *, tq, n_cmp_pad, n_slc, top_k):
    qi = pl.program_id(1)
    t0 = qi * tq
    hq = NSA_HEADS * tq
    q_t = jnp.concatenate([qt_ref[0, h * HEAD_DIM:(h + 1) * HEAD_DIM, :] for h in range(NSA_HEADS)], axis=1)

    def lane_query(shape):
        return lax.broadcasted_iota(jnp.int32, shape, 1) & (tq - 1)

    s_c = _dot(kc_ref[0], q_t)
    n_id = lax.broadcasted_iota(jnp.int32, (n_cmp_pad, hq), 0)
    valid_c = (n_id * CMP_STRIDE + (CMP_LEN - 1)) <= t0 + lane_query((n_cmp_pad, hq))
    m_c = jnp.max(jnp.where(valid_c, s_c, NEG), 0, keepdims=True)
    e_c = jnp.where(valid_c, jnp.exp2(s_c - m_c), 0.0)
    l_c = jnp.sum(e_c, 0, keepdims=True)
    p_c = e_c * (1.0 / jnp.where(l_c > 0.0, l_c, 1.0))
    o_c = _dot(vct_ref[0], p_c.astype(BF16))

    p_sum = p_c[:, 0:tq]
    for h in range(1, NSA_HEADS):
        p_sum = p_sum + p_c[:, h * tq:(h + 1) * tq]
    oj = lax.broadcasted_iota(jnp.int32, (LANES, n_cmp_pad), 0) * SLC_BLOCK
    on = lax.broadcasted_iota(jnp.int32, (LANES, n_cmp_pad), 1) * CMP_STRIDE
    overlap_t = jnp.where((on < oj + SLC_BLOCK) & (on + CMP_LEN > oj), 1.0, 0.0).astype(BF16)
    imp = sum(_dot(overlap_t, part) for part in _split3(p_sum))

    j = lax.broadcasted_iota(jnp.int32, (LANES, tq), 0)
    cur = (t0 + lax.broadcasted_iota(jnp.int32, (LANES, tq), 1)) // SLC_BLOCK
    forced = (j == 0) | (j == cur) | (j == cur - 1)
    cand = jnp.where(jnp.logical_not(forced) & (j <= cur) & (j < n_slc), imp, -1.0)
    sel = jnp.where(forced, 1.0, 0.0)
    for _ in range(max(top_k - 3, 0)):
        best = jnp.max(cand, 0, keepdims=True)
        idx = jnp.min(jnp.where(cand == best, j, LANES), 0, keepdims=True)
        hit = j == idx
        sel = jnp.where(hit & (best >= 0.0), 1.0, sel)
        cand = jnp.where(hit, -2.0, cand)
    bias = jnp.where(sel > 0.0, 0.0, NEG).astype(BF16)
    q_aug = jnp.concatenate([q_t, jnp.concatenate([bias] * NSA_HEADS, axis=1)], axis=0)

    def flash_init(m_ref, l_ref, acc_ref):
        m_ref[...] = jnp.full(m_ref.shape, NEG, F32)
        l_ref[...] = jnp.zeros(l_ref.shape, F32)
        acc_ref[...] = jnp.zeros(acc_ref.shape, F32)

    def flash_step(m_ref, l_ref, acc_ref, k_tile, vt_tile, q_op, mask):
        s = _dot(k_tile, q_op)
        if mask is not None:
            s = jnp.where(mask, s, NEG)
        m_old = m_ref[...]
        m_new = jnp.maximum(m_old, jnp.max(s, 0, keepdims=True))
        alpha = jnp.exp2(m_old - m_new)
        p = jnp.exp2(s - m_new)
        l_ref[...] = alpha * l_ref[...] + jnp.sum(p, 0, keepdims=True)
        acc_ref[...] = alpha * acc_ref[...] + _dot(vt_tile, p.astype(BF16))
        m_ref[...] = m_new

    kl = lax.broadcasted_iota(jnp.int32, (tq, hq), 0)
    ql = lane_query((tq, hq))

    flash_init(m_sc, l_sc, acc_sc)

    def full_tile(kb, carry):
        off = pl.multiple_of(kb * tq, tq)
        flash_step(m_sc, l_sc, acc_sc, ks_ref[0, pl.ds(off, tq), :], vst_ref[0, kb], q_aug, None)
        return carry

    lax.fori_loop(0, qi, full_tile, 0)
    off_d = pl.multiple_of(t0, tq)
    flash_step(m_sc, l_sc, acc_sc, ks_ref[0, pl.ds(off_d, tq), :], vst_ref[0, qi], q_aug, kl <= ql)

    flash_init(m2_sc, l2_sc, acc2_sc)
    flash_step(m2_sc, l2_sc, acc2_sc, kw_ref[0, pl.ds(off_d, tq), :], vwt_ref[0, qi], q_t, kl <= ql)
    for back in range(1, WINDOW // tq + 1):
        @pl.when(qi >= back)
        def _():
            off = pl.multiple_of(t0 - back * tq, tq)
            mask = (kl > ql) if back * tq == WINDOW else None
            flash_step(m2_sc, l2_sc, acc2_sc, kw_ref[0, pl.ds(off, tq), :], vwt_ref[0, qi - back], q_t, mask)

    o_s = acc_sc[...] * (1.0 / l_sc[...])
    o_w = acc2_sc[...] * (1.0 / l2_sc[...])
    gates_t = _sigmoid(sm_ref[0]).T
    for h in range(NSA_HEADS):
        cols = slice(h * tq, (h + 1) * tq)
        c0 = SM_G + 3 * h
        o_t = (gates_t[c0:c0 + 1, :] * o_c[:, cols] + gates_t[c0 + 1:c0 + 2, :] * o_s[:, cols]
               + gates_t[c0 + 2:c0 + 3, :] * o_w[:, cols])
        o_ref[0, :, h * HEAD_DIM:(h + 1) * HEAD_DIM] = o_t.T.astype(o_ref.dtype)


def _nsa_attention(q_t, small3, k_cmp, v_cmp_t, ks_aug, vs_t, kw_rope, vw_t, tq):
    b, _, s = q_t.shape
    n_cmp_pad = k_cmp.shape[1]
    n_slc = s // SLC_BLOCK
    top_k = min(SLC_TOPK, n_slc)
    hq = NSA_HEADS * tq
    assert WINDOW % tq == 0 and vs_t.shape[-1] == tq
    batch_spec = lambda shape: pl.BlockSpec((1,) + shape, lambda bi, i: (bi,) + (0,) * len(shape))
    flash_scratch = [pltpu.VMEM((1, hq), F32), pltpu.VMEM((1, hq), F32), pltpu.VMEM((HEAD_DIM, hq), F32)]
    return pl.pallas_call(
        functools.partial(_nsa_attn_body, tq=tq, n_cmp_pad=n_cmp_pad, n_slc=n_slc, top_k=top_k),
        grid=(b, s // tq),
        in_specs=[
            pl.BlockSpec((1, NSA_Q, tq), lambda bi, i: (bi, 0, i)),
            pl.BlockSpec((1, tq, LANES), lambda bi, i: (bi, i, 0)),
            batch_spec((n_cmp_pad, HEAD_DIM)),
            batch_spec((HEAD_DIM, n_cmp_pad)),
            batch_spec((s, 2 * HEAD_DIM)),
            batch_spec((s // tq, HEAD_DIM, tq)),
            batch_spec((s, HEAD_DIM)),
            batch_spec((s // tq, HEAD_DIM, tq)),
        ],
        out_specs=pl.BlockSpec((1, tq, NSA_Q), lambda bi, i: (bi, i, 0)),
        out_shape=jax.ShapeDtypeStruct((b, s, NSA_Q), BF16),
        scratch_shapes=flash_scratch + flash_scratch,
        compiler_params=_cparams("arbitrary", "arbitrary"),
        name="nsa_attn",
    )(q_t, small3, k_cmp, v_cmp_t, ks_aug, vs_t, kw_rope, vw_t)


def _gdn_body(xq_ref, xk_ref, xv_ref, z_ref, sm_ref, cw_ref, hp_ref, nw_ref, o_ref,
              tail_sc, state_sc, qd_sc, kt_sc, u_sc, w_sc, at_sc, eg_sc, *, ct):
    i = pl.program_id(1)
    c = GDN_CHUNK
    nc = ct // c

    @pl.when(i == 0)
    def _():
        tail_sc[...] = jnp.zeros_like(tail_sc)
        state_sc[...] = jnp.zeros_like(state_sc)

    row = lax.broadcasted_iota(jnp.int32, (ct, LANES), 0)
    row_in_chunk = row & (c - 1)
    ii = lax.broadcasted_iota(jnp.int32, (nc, c, c), 1)
    jj = lax.broadcasted_iota(jnp.int32, (nc, c, c), 2)

    def conv_silu(x_ref, part, h):
        col = slice(h * HEAD_DIM, (h + 1) * HEAD_DIM)
        wcol = slice(part * GDN_QK + h * HEAD_DIM, part * GDN_QK + (h + 1) * HEAD_DIM)
        x = x_ref[0, :, col].astype(F32)
        xcat = jnp.concatenate([tail_sc[part, :, col], x], axis=0)
        acc = xcat[8:] * cw_ref[GDN_CONV - 1:GDN_CONV, wcol]
        for tap in range(GDN_CONV - 1):
            shift = GDN_CONV - 1 - tap
            acc = acc + pltpu.roll(xcat, shift, 0)[8:] * cw_ref[tap:tap + 1, wcol]
        tail_sc[part, :, col] = x[ct - 8:]
        return _silu(acc)

    def l2n(v):
        return v * lax.rsqrt(jnp.sum(v * v, -1, keepdims=True) + RMS_EPS)

    sm = sm_ref[0]
    for h in range(GDN_HEADS):
        q = l2n(conv_silu(xq_ref, 0, h)) * (HEAD_DIM ** -0.5)
        k = l2n(conv_silu(xk_ref, 1, h))
        v = conv_silu(xv_ref, 2, h)
        a = sm[:, SM_A + h:SM_A + h + 1]
        bcol = sm[:, SM_B + h:SM_B + h + 1]
        a_log = hp_ref[0:1, h:h + 1]
        dt_bias = hp_ref[1:2, h:h + 1]
        sp_in = a + dt_bias
        softplus = jnp.maximum(sp_in, 0.0) + jnp.log(1.0 + jnp.exp(-jnp.abs(sp_in)))
        g = -jnp.exp(a_log) * softplus
        beta = _sigmoid(bcol)
        gc = jnp.broadcast_to(g, (ct, LANES))
        step = 1
        while step < c:
            gc = gc + jnp.where(row_in_chunk >= step, pltpu.roll(gc, step, 0), 0.0)
            step *= 2
        egc = jnp.exp(gc)
        kb = k * beta
        vb = v * beta
        gc3 = gc.reshape(nc, c, LANES)
        g_col = gc3[:, :, :c]
        g_row = jnp.swapaxes(gc3, 1, 2)[:, :c, :]
        decay = jnp.where(ii >= jj, jnp.exp(jnp.minimum(g_col - g_row, 0.0)), 0.0)
        k3 = k.reshape(nc, c, LANES).astype(BF16)
        kk = jnp.einsum("nid,njd->nij", kb.reshape(nc, c, LANES).astype(BF16), k3, preferred_element_type=F32)
        a_mat = jnp.where(ii > jj, kk * decay, 0.0)
        n_mat = -a_mat
        pw = a_mat
        sq = 2
        while sq < c:
            pwb = pw.astype(BF16)
            pw = jnp.einsum("nij,njk->nik", pwb, pwb, preferred_element_type=F32)
            n_mat = n_mat + pw + jnp.einsum("nij,njk->nik", n_mat.astype(BF16), pw.astype(BF16),
                                            preferred_element_type=F32)
            sq *= 2
        nb = n_mat.astype(BF16)
        vb3 = vb.reshape(nc, c, LANES)
        kbg3 = (kb * egc).reshape(nc, c, LANES)
        u = vb3 + jnp.einsum("nij,njd->nid", nb, vb3.astype(BF16), preferred_element_type=F32)
        w = kbg3 + jnp.einsum("nij,njd->nid", nb, kbg3.astype(BF16), preferred_element_type=F32)
        attn = jnp.einsum("nid,njd->nij", q.reshape(nc, c, LANES).astype(BF16), k3,
                          preferred_element_type=F32) * decay
        g_last = jnp.broadcast_to(gc3[:, c - 1:c, :], (nc, c, LANES))
        qd_sc[h] = (q * egc).astype(BF16)
        kt_sc[h] = (k.reshape(nc, c, LANES) * jnp.exp(g_last - gc3)).reshape(ct, LANES)
        u_sc[h] = u.reshape(ct, LANES)
        w_sc[h] = w.reshape(ct, LANES).astype(BF16)
        at_sc[h] = attn.reshape(ct, c).astype(BF16)
        eg_sc[h] = jnp.exp(g_last).reshape(ct, LANES)

    nw = nw_ref[...]

    def chunk_step(n, carry):
        off = pl.multiple_of(n * c, c)
        rows = pl.ds(off, c)
        for h in range(GDN_HEADS):
            state = state_sc[h]
            sb = state.astype(BF16)
            v_new = u_sc[h, rows, :] - _dot(w_sc[h, rows, :], sb)
            vnb = v_new.astype(BF16)
            o = _dot(qd_sc[h, rows, :], sb) + _dot(at_sc[h, rows, :], vnb)
            decay_last = eg_sc[h, pl.ds(off, 1), :]
            kt_t = kt_sc[h, rows, :].T.astype(BF16)
            state_sc[h] = state * decay_last + _dot(kt_t, vnb)
            zf = z_ref[0, rows, h * HEAD_DIM:(h + 1) * HEAD_DIM].astype(F32)
            o = o * lax.rsqrt(jnp.mean(o * o, -1, keepdims=True) + RMS_EPS) * nw * _silu(zf)
            o_ref[0, rows, h * HEAD_DIM:(h + 1) * HEAD_DIM] = o.astype(o_ref.dtype)
        return carry

    lax.fori_loop(0, nc, chunk_step, 0)


def _gdn(proj3, small3, conv_w, hp, norm_w, ct):
    b, s, _ = proj3.shape
    cb = lambda col: col // GDN_QK
    blk = lambda col: pl.BlockSpec((1, ct, GDN_QK), lambda bi, i: (bi, i, cb(col)))
    hsh = (GDN_HEADS, ct, LANES)
    return pl.pallas_call(
        functools.partial(_gdn_body, ct=ct),
        grid=(b, s // ct),
        in_specs=[blk(COL_GQ), blk(COL_GK), blk(COL_GV), blk(COL_Z),
                  pl.BlockSpec((1, ct, LANES), lambda bi, i: (bi, i, 0)),
                  _resident(conv_w.shape), _resident(hp.shape), _resident(norm_w.shape)],
        out_specs=pl.BlockSpec((1, ct, GDN_V), lambda bi, i: (bi, i, 0)),
        out_shape=jax.ShapeDtypeStruct((b, s, GDN_V), BF16),
        scratch_shapes=[
            pltpu.VMEM((3, 8, GDN_QK), F32),
            pltpu.VMEM((GDN_HEADS, HEAD_DIM, HEAD_DIM), F32),
            pltpu.VMEM(hsh, BF16),
            pltpu.VMEM(hsh, F32),
            pltpu.VMEM(hsh, F32),
            pltpu.VMEM(hsh, BF16),
            pltpu.VMEM((GDN_HEADS, ct, GDN_CHUNK), BF16),
            pltpu.VMEM(hsh, F32),
        ],
        compiler_params=_cparams("arbitrary", "arbitrary"),
        name="gdn",
    )(proj3, proj3, proj3, proj3, small3, conv_w, hp, norm_w)


def _merge_body(x_ref, oa_ref, ob_ref, ga_ref, gb_ref, wa_ref, wb_ref, wm_ref, g_ref, b_ref, o_ref):
    ya = _dot(oa_ref[...], wa_ref[...])
    yb = _dot(ob_ref[...], wb_ref[...])
    y = _sigmoid(ga_ref[...].astype(F32)) * ya + _sigmoid(gb_ref[...].astype(F32)) * yb
    out = _dot(y.astype(BF16), wm_ref[...])
    o_ref[...] = _layer_norm(DN_ALPHA * x_ref[...] + out, g_ref[...], b_ref[...])


def _merge(x2d, oa, ob, proj2d, wa, wb, wm, g, bias, tm):
    t, d = x2d.shape
    row = lambda w, cbi=0: pl.BlockSpec((tm, w), lambda i: (i, cbi))
    return pl.pallas_call(
        _merge_body,
        grid=(t // tm,),
        in_specs=[row(d), row(GDN_V), row(NSA_Q), row(d, COL_GATE_A // d), row(d, COL_GATE_B // d),
                  _resident(wa.shape), _resident(wb.shape), _resident(wm.shape), _resident(g.shape),
                  _resident(bias.shape)],
        out_specs=row(d),
        out_shape=jax.ShapeDtypeStruct((t, d), F32),
        compiler_params=_cparams("arbitrary"),
        name="merge",
    )(x2d, oa, ob, proj2d, proj2d, wa, wb, wm, g, bias)


def _memkv_body(m_ref, wk_ref, wv_ref, k_ref, v_ref):
    m = m_ref[...].astype(BF16)
    k_ref[...] = _dot(m, wk_ref[...]).astype(BF16)
    v_ref[...] = _dot(m, wv_ref[...]).astype(BF16)


def _memkv(mem2d, wk, wv, tm):
    t, d = mem2d.shape
    row = pl.BlockSpec((tm, d), lambda i: (i, 0))
    return pl.pallas_call(
        _memkv_body,
        grid=(t // tm,),
        in_specs=[row, _resident(wk.shape), _resident(wv.shape)],
        out_specs=[row, row],
        out_shape=[jax.ShapeDtypeStruct((t, d), BF16)] * 2,
        compiler_params=_cparams("arbitrary"),
        name="memkv",
    )(mem2d, wk, wv)


def _xattn_body(x_ref, k_ref, v_ref, wq_ref, wo_ref, g_ref, b_ref, o_ref):
    x = x_ref[0]
    q = _dot(x.astype(BF16), wq_ref[...]).astype(BF16)
    heads = []
    for h in range(XA_HEADS):
        sl = slice(h * XA_HEAD_DIM, (h + 1) * XA_HEAD_DIM)
        s = _dot_nt(q[:, sl], k_ref[0, :, sl]) * (XA_HEAD_DIM ** -0.5)
        m = jnp.max(s, -1, keepdims=True)
        e = jnp.exp(s - m)
        p = e * (1.0 / jnp.sum(e, -1, keepdims=True))
        heads.append(_dot(p.astype(BF16), v_ref[0, :, sl]).astype(BF16))
    o = jnp.concatenate(heads, axis=1)
    out = _dot(o, wo_ref[...])
    o_ref[0] = _layer_norm(DN_ALPHA * x + out, g_ref[...], b_ref[...])


def _xattn(x3, k3, v3, wq, wo, g, bias, tm):
    b, s, d = x3.shape
    m = k3.shape[1]
    return pl.pallas_call(
        _xattn_body,
        grid=(b, s // tm),
        in_specs=[pl.BlockSpec((1, tm, d), lambda bi, i: (bi, i, 0)),
                  pl.BlockSpec((1, m, d), lambda bi, i: (bi, 0, 0)),
                  pl.BlockSpec((1, m, d), lambda bi, i: (bi, 0, 0)),
                  _resident(wq.shape), _resident(wo.shape), _resident(g.shape), _resident(bias.shape)],
        out_specs=pl.BlockSpec((1, tm, d), lambda bi, i: (bi, i, 0)),
        out_shape=jax.ShapeDtypeStruct((b, s, d), F32),
        compiler_params=_cparams("arbitrary", "arbitrary"),
        name="xattn",
    )(x3, k3, v3, wq, wo, g, bias)


def _ffn_body(x_ref, wg_ref, wu_ref, wd_ref, g_ref, b_ref, o_ref, *, chunk):
    x = x_ref[...]
    xb = x.astype(BF16)
    out = None
    for c in range(0, wg_ref.shape[1], chunk):
        hid = _silu(_dot(xb, wg_ref[:, c:c + chunk])) * _dot(xb, wu_ref[:, c:c + chunk])
        part = _dot(hid.astype(BF16), wd_ref[c:c + chunk, :])
        out = part if out is None else out + part
    o_ref[...] = _layer_norm(DN_ALPHA * x + out, g_ref[...], b_ref[...])


def _ffn(x2d, wg, wu, wd, g, bias, tm):
    t, d = x2d.shape
    dff = wg.shape[1]
    chunk = dff // 2 if (dff // 2) % LANES == 0 else dff
    row = pl.BlockSpec((tm, d), lambda i: (i, 0))
    return pl.pallas_call(
        functools.partial(_ffn_body, chunk=chunk),
        grid=(t // tm,),
        in_specs=[row, _resident(wg.shape), _resident(wu.shape), _resident(wd.shape), _resident(g.shape),
                  _resident(bias.shape)],
        out_specs=row,
        out_shape=jax.ShapeDtypeStruct((t, d), F32),
        compiler_params=_cparams("arbitrary"),
        name="ffn",
    )(x2d, wg, wu, wd, g, bias)


def _tile(n, pref):
    t = min(n, pref)
    assert n % t == 0, (n, pref)
    return t


def _layer(x, mem, positions, w_in, conv_w, a_log, dt_bias, gdn_norm_w, cmp_pos_k, cmp_pos_v, cmp_k_w1, cmp_k_w2,
           cmp_v_w1, cmp_v_w2, w_up_gdn, w_up_nsa, w_mix_out, ln1_g, ln1_b, xa_wq, xa_wk, xa_wv, xa_wo, ln2_g,
           ln2_b, ffn_w_gate, ffn_w_up, ffn_w_down, ln3_g, ln3_b):
    b, s, d = x.shape
    t = b * s
    assert d == D_MODEL and s % WINDOW == 0 and s // SLC_BLOCK <= LANES and s >= 2 * WINDOW

    o_qkv, o_z = 0, 3 * GDN_QK
    o_a = o_z + GDN_V
    o_b = o_a + GDN_HEADS
    o_qn = o_b + GDN_HEADS
    o_kv6 = o_qn + NSA_Q
    o_gn = o_kv6 + 6 * HEAD_DIM
    o_ga = o_gn + 3 * NSA_HEADS
    o_gb = o_ga + D_MODEL
    w_main = jnp.concatenate([w_in[:, o_ga:o_gb], w_in[:, o_gb:o_gb + D_MODEL], w_in[:, o_qkv:o_z], w_in[:, o_z:o_a],
                              w_in[:, o_qn:o_kv6], w_in[:, o_kv6:o_gn]], axis=1).astype(BF16)
    n_small = 2 * GDN_HEADS + 3 * NSA_HEADS
    w_small = jnp.concatenate([w_in[:, o_a:o_qn], w_in[:, o_gn:o_ga], jnp.zeros((d, LANES - n_small), w_in.dtype)],
                              axis=1).astype(BF16)

    x2d = x.reshape(t, d)
    proj2d, small2d = _project(x2d, w_main, w_small, _tile(t, 512))
    proj3 = proj2d.reshape(b, s, N_MAIN)
    small3 = small2d.reshape(b, s, LANES)

    hp = jnp.zeros((8, LANES), F32).at[0, :GDN_HEADS].set(a_log.astype(F32)).at[1, :GDN_HEADS].set(
        dt_bias.astype(F32))
    o_gdn = _gdn(proj3, small3, conv_w.astype(F32), hp, gdn_norm_w.reshape(1, HEAD_DIM).astype(F32), _tile(s, 512))

    half = HEAD_DIM // 2
    inv = ROPE_THETA ** (-jnp.arange(half, dtype=F32) / half)
    inv_row = jnp.concatenate([inv, inv]).reshape(1, HEAD_DIM)
    posf = positions.astype(F32)
    tq = _tile(s, 256)
    q_t, ks_aug, vs_t, kw_rope, vw_t = _nsa_prep(proj3, posf.reshape(b, s, 1), inv_row, _tile(s, 512), tq)

    n_seg = s // CMP_STRIDE
    seg_w = CMP_STRIDE * HEAD_DIM
    kc_seg = proj3[:, :, COL_KV6:COL_KV6 + HEAD_DIM].reshape(b, n_seg, seg_w)
    vc_seg = proj3[:, :, COL_KV6 + HEAD_DIM:COL_KV6 + 2 * HEAD_DIM].reshape(b, n_seg, seg_w)
    posc = posf[:, CMP_LEN - 1::CMP_STRIDE]
    posc = jnp.concatenate([posc, posc[:, -1:]], axis=1).reshape(b, n_seg, 1)
    k_cmp, v_cmp_t = _compress(kc_seg, vc_seg, cmp_pos_k.reshape(1, CMP_LEN * HEAD_DIM).astype(F32),
                             cmp_pos_v.reshape(1, CMP_LEN * HEAD_DIM).astype(F32), cmp_k_w1.astype(BF16),
                             cmp_k_w2.astype(BF16), cmp_v_w1.astype(BF16), cmp_v_w2.T.astype(BF16), posc, inv_row)
    o_nsa = _nsa_attention(q_t, small3, k_cmp, v_cmp_t, ks_aug, vs_t, kw_rope, vw_t, tq)

    vec = lambda p: p.reshape(1, d).astype(F32)
    x1 = _merge(x2d, o_gdn.reshape(t, GDN_V), o_nsa.reshape(t, NSA_Q), proj2d, w_up_gdn.astype(BF16),
                w_up_nsa.astype(BF16), w_mix_out.astype(BF16), vec(ln1_g), vec(ln1_b), _tile(t, 512))
    mlen = mem.shape[1]
    mk, mv = _memkv(mem.reshape(b * mlen, d), xa_wk.astype(BF16), xa_wv.astype(BF16), _tile(b * mlen, 512))
    x2 = _xattn(x1.reshape(b, s, d), mk.reshape(b, mlen, d), mv.reshape(b, mlen, d), xa_wq.astype(BF16),
                xa_wo.astype(BF16), vec(ln2_g), vec(ln2_b), _tile(s, 512))
    x3 = _ffn(x2.reshape(t, d), ffn_w_gate.astype(BF16), ffn_w_up.astype(BF16), ffn_w_down.astype(BF16),
              vec(ln3_g), vec(ln3_b), _tile(t, 512))
    return x3.reshape(b, s, d)


def kernel(x, mem, positions, w_in, gdn_conv_w, gdn_a_log, gdn_dt_bias, gdn_norm_w, cmp_pos_k, cmp_pos_v, cmp_k_w1, cmp_k_w2, cmp_v_w1, cmp_v_w2, w_up_gdn, w_up_nsa, w_mix_out, ln1_g, ln1_b, xa_wq, xa_wk, xa_wv, xa_wo, ln2_g, ln2_b, ffn_w_gate, ffn_w_up, ffn_w_down, ln3_g, ln3_b):
    for l in range(w_in.shape[0]):
        x = _layer(x, mem, positions, w_in[l], gdn_conv_w[l], gdn_a_log[l], gdn_dt_bias[l], gdn_norm_w[l],
                   cmp_pos_k[l], cmp_pos_v[l], cmp_k_w1[l], cmp_k_w2[l], cmp_v_w1[l], cmp_v_w2[l], w_up_gdn[l],
                   w_up_nsa[l], w_mix_out[l], ln1_g[l], ln1_b[l], xa_wq[l], xa_wk[l], xa_wv[l], xa_wo[l], ln2_g[l],
                   ln2_b[l], ffn_w_gate[l], ffn_w_up[l], ffn_w_down[l], ln3_g[l], ln3_b[l])
    return x
```

```python
import functools

import jax
import jax.numpy as jnp
from jax import lax
from jax.experimental import pallas as pl
from jax.experimental.pallas import tpu as pltpu

F32 = jnp.float32
BF16 = jnp.bfloat16

D_MODEL = 1024
HEAD_DIM = 128
GDN_HEADS = 4
GDN_QK = GDN_HEADS * HEAD_DIM
GDN_V = GDN_HEADS * HEAD_DIM
GDN_CONV = 4
GDN_CHUNK = 64
NSA_HEADS = 4
NSA_Q = NSA_HEADS * HEAD_DIM
CMP_LEN = 32
CMP_STRIDE = 16
SLC_BLOCK = 64
SLC_TOPK = 16
WINDOW = 512
FORCE_SCORE = 1.0e4
XA_HEADS = 4
XA_HEAD_DIM = D_MODEL // XA_HEADS
ROPE_THETA = 10000.0
LN_EPS = 1e-5
RMS_EPS = 1e-6
NEG = -1.0e30
ATTN_SCALE = HEAD_DIM ** -0.5
LOG2E = 1.4426950408889634
DEPTH = 1
DN_ALPHA = (2 * DEPTH) ** 0.25

LANES = 128
VT_ONES_ROWS = 16
VMEM_LIMIT_BYTES = 56 * 1024 * 1024

COL_GATE_A = 0
COL_GATE_B = 1024
COL_GQ = 2048
COL_GK = 2560
COL_GV = 3072
COL_Z = 3584
COL_NQ = 4096
COL_KV6 = 4608
N_MAIN = 5376
SM_A = 0
SM_B = 4
SM_G = 8


def _cparams(*sem):
    return pltpu.CompilerParams(dimension_semantics=sem, vmem_limit_bytes=VMEM_LIMIT_BYTES)


def _resident(shape):
    nd = len(shape)
    return pl.BlockSpec(shape, lambda *_: (0,) * nd, pipeline_mode=pl.Buffered(1))


def _layer_norm(v, g, b):
    mu = jnp.mean(v, -1, keepdims=True)
    c = v - mu
    var = jnp.mean(c * c, -1, keepdims=True)
    return c * lax.rsqrt(var + LN_EPS) * g + b


def _silu(v):
    return v * (1.0 / (1.0 + jnp.exp(-v)))


def _sigmoid(v):
    return 1.0 / (1.0 + jnp.exp(-v))


def _dot(a, b):
    return jnp.dot(a, b, preferred_element_type=F32)


def _dot_nt(a, b):
    return lax.dot_general(a, b, (((1,), (1,)), ((), ())), preferred_element_type=F32)


def _proj_body(x_ref, w_ref, ws_ref, o_ref, os_ref, *, chunk):
    x = x_ref[...].astype(BF16)
    for c in range(0, w_ref.shape[1], chunk):
        o_ref[:, c:c + chunk] = _dot(x, w_ref[:, c:c + chunk]).astype(o_ref.dtype)
    os_ref[...] = _dot(x, ws_ref[...])


def _project(x2d, w_main, w_small, tm):
    t, d = x2d.shape
    n = w_main.shape[1]
    return pl.pallas_call(
        functools.partial(_proj_body, chunk=768),
        grid=(t // tm,),
        in_specs=[pl.BlockSpec((tm, d), lambda i: (i, 0)), _resident((d, n)), _resident((d, LANES))],
        out_specs=[pl.BlockSpec((tm, n), lambda i: (i, 0)), pl.BlockSpec((tm, LANES), lambda i: (i, 0))],
        out_shape=[jax.ShapeDtypeStruct((t, n), BF16), jax.ShapeDtypeStruct((t, LANES), F32)],
        compiler_params=_cparams("arbitrary"),
        name="proj",
    )(x2d, w_main, w_small)


def _rope_tables(pos_col, inv_row):
    ang = pos_col * inv_row
    cos = jnp.cos(ang)
    sin = jnp.sin(ang)
    lane = lax.broadcasted_iota(jnp.int32, ang.shape, 1)
    return cos, jnp.where(lane < HEAD_DIM // 2, -sin, sin)


def _rope(v, cos, sin_signed):
    return v * cos + pltpu.roll(v, HEAD_DIM // 2, 1) * sin_signed


def _nsa_prep_body(pos_ref, inv_ref, q_ref, ks_ref, vs_ref, kw_ref, vw_ref,
                   qt_ref, kso_ref, vst_ref, kwo_ref, vwt_ref, *, ct, tk):
    i = pl.program_id(1)
    cos, sin_s = _rope_tables(pos_ref[0], inv_ref[...])
    for h in range(NSA_HEADS):
        sl = slice(h * HEAD_DIM, (h + 1) * HEAD_DIM)
        q_rot = _rope(q_ref[0, :, sl].astype(F32), cos, sin_s) * (ATTN_SCALE * LOG2E)
        qt_ref[0, sl, :] = q_rot.T.astype(BF16)
    kso_ref[0, :, :HEAD_DIM] = _rope(ks_ref[0].astype(F32), cos, sin_s).astype(BF16)
    lane = lax.broadcasted_iota(jnp.int32, (ct, LANES), 1)
    tok = lax.broadcasted_iota(jnp.int32, (ct, LANES), 0) + i * ct
    kso_ref[0, :, HEAD_DIM:] = jnp.where(lane == tok // SLC_BLOCK, 1.0, 0.0).astype(BF16)
    kwo_ref[0] = _rope(kw_ref[0].astype(F32), cos, sin_s).astype(BF16)
    ones = jnp.ones((VT_ONES_ROWS, tk), BF16)
    for v_ref, vt_ref in ((vs_ref, vst_ref), (vw_ref, vwt_ref)):
        v_t = v_ref[0].astype(F32).T
        for n in range(ct // tk):
            vt_ref[0, n, :HEAD_DIM, :] = v_t[:, n * tk:(n + 1) * tk].astype(BF16)
            vt_ref[0, n, HEAD_DIM:, :] = ones


def _nsa_prep(proj3, posf, inv_row, ct, tk):
    b, s, _ = proj3.shape
    kv_col = COL_KV6 // HEAD_DIM
    col = lambda c: pl.BlockSpec((1, ct, HEAD_DIM), lambda bi, i: (bi, i, c))
    vrows = HEAD_DIM + VT_ONES_ROWS
    vt_spec = pl.BlockSpec((1, ct // tk, vrows, tk), lambda bi, i: (bi, i, 0, 0))
    vt_shape = jax.ShapeDtypeStruct((b, s // tk, vrows, tk), BF16)
    return pl.pallas_call(
        functools.partial(_nsa_prep_body, ct=ct, tk=tk),
        grid=(b, s // ct),
        in_specs=[
            pl.BlockSpec((1, ct, 1), lambda bi, i: (bi, i, 0)),
            _resident((1, LANES)),
            pl.BlockSpec((1, ct, NSA_Q), lambda bi, i: (bi, i, COL_NQ // NSA_Q)),
            col(kv_col + 2), col(kv_col + 3), col(kv_col + 4), col(kv_col + 5),
        ],
        out_specs=[
            pl.BlockSpec((1, NSA_Q, ct), lambda bi, i: (bi, 0, i)),
            pl.BlockSpec((1, ct, 2 * HEAD_DIM), lambda bi, i: (bi, i, 0)),
            vt_spec,
            pl.BlockSpec((1, ct, HEAD_DIM), lambda bi, i: (bi, i, 0)),
            vt_spec,
        ],
        out_shape=[
            jax.ShapeDtypeStruct((b, NSA_Q, s), BF16),
            jax.ShapeDtypeStruct((b, s, 2 * HEAD_DIM), BF16),
            vt_shape,
            jax.ShapeDtypeStruct((b, s, HEAD_DIM), BF16),
            vt_shape,
        ],
        compiler_params=_cparams("arbitrary", "arbitrary"),
        name="nsa_prep",
    )(posf, inv_row, proj3, proj3, proj3, proj3, proj3)


def _compress_body(kc_ref, vc_ref, pk_ref, pv_ref, kw1_ref, kw2_ref, vw1_ref, vw2_ref, posc_ref, inv_ref,
                   ko_ref, vo_ref):
    def comp(seg_ref, p_ref, w1_ref, w2_ref, transposed):
        seg = seg_ref[0]
        n_seg, half = seg.shape
        first = _dot(seg, w1_ref[:half, :])
        second = _dot(seg, w1_ref[half:, :])
        pos_term = _dot(jnp.broadcast_to(p_ref[...], (8, p_ref.shape[1])).astype(BF16), w1_ref[...])[0:1]
        hidden = first + pltpu.roll(second, n_seg - 1, 0) + pos_term
        act = _silu(hidden)
        if transposed:
            return _dot(w2_ref[...], act.T.astype(BF16))
        return _dot(act.astype(BF16), w2_ref[...])

    kc = comp(kc_ref, pk_ref, kw1_ref, kw2_ref, False)
    cos, sin_s = _rope_tables(posc_ref[0], inv_ref[...])
    ko_ref[0] = _rope(kc, cos, sin_s).astype(BF16)
    vo_ref[0] = comp(vc_ref, pv_ref, vw1_ref, vw2_ref, True).astype(BF16)


def _compress(kc_seg, vc_seg, pos_k, pos_v, kw1, kw2, vw1, vw2, posc, inv_row):
    b, n_seg, seg_w = kc_seg.shape
    seg_spec = pl.BlockSpec((1, n_seg, seg_w), lambda bi: (bi, 0, 0))
    return pl.pallas_call(
        _compress_body,
        grid=(b,),
        in_specs=[seg_spec, seg_spec, _resident(pos_k.shape), _resident(pos_v.shape), _resident(kw1.shape),
                  _resident(kw2.shape), _resident(vw1.shape), _resident(vw2.shape),
                  pl.BlockSpec((1, n_seg, 1), lambda bi: (bi, 0, 0)), _resident((1, LANES))],
        out_specs=[pl.BlockSpec((1, n_seg, HEAD_DIM), lambda bi: (bi, 0, 0)),
                   pl.BlockSpec((1, HEAD_DIM, n_seg), lambda bi: (bi, 0, 0))],
        out_shape=[jax.ShapeDtypeStruct((b, n_seg, HEAD_DIM), BF16),
                   jax.ShapeDtypeStruct((b, HEAD_DIM, n_seg), BF16)],
        compiler_params=_cparams("arbitrary"),
        name="compress",
    )(kc_seg, vc_seg, pos_k, pos_v, kw1, kw2, vw1, vw2, posc, inv_row)


def _split3(v):
    hi = v.astype(BF16)
    r1 = v - hi.astype(F32)
    mid = r1.astype(BF16)
    lo = (r1 - mid.astype(F32)).astype(BF16)
    return hi, mid, lo


def _nsa_attn_body(qt_ref, sm_ref, kc_ref, vct_ref, ks_ref, vst_ref, kw_ref, vwt_ref, o_ref,
                   m_sc, acc_sc, m2_sc, acc2_sc, s_buf, *, tq, n_cmp_pad, n_slc, top_k):
    qi = pl.program_id(1)
    t0 = qi * tq
    hq = NSA_HEADS * tq
    q_t = jnp.concatenate([qt_ref[0, h * HEAD_DIM:(h + 1) * HEAD_DIM, :] for h in range(NSA_HEADS)], axis=1)

    def lane_query(shape):
        return lax.broadcasted_iota(jnp.int32, shape, 1) & (tq - 1)

    s_c = _dot(kc_ref[0], q_t)
    n_id = lax.broadcasted_iota(jnp.int32, (n_cmp_pad, hq), 0)
    valid_c = (n_id * CMP_STRIDE + (CMP_LEN - 1)) <= t0 + lane_query((n_cmp_pad, hq))
    m_c = jnp.max(jnp.where(valid_c, s_c, NEG), 0, keepdims=True)
    e_c = jnp.where(valid_c, jnp.exp2(s_c - m_c), 0.0)
    l_c = jnp.sum(e_c, 0, keepdims=True)
    p_c = e_c * (1.0 / jnp.where(l_c > 0.0, l_c, 1.0))
    o_c = _dot(vct_ref[0], p_c.astype(BF16))

    p_sum = p_c[:, 0:tq]
    for h in range(1, NSA_HEADS):
        p_sum = p_sum + p_c[:, h * tq:(h + 1) * tq]
    oj = lax.broadcasted_iota(jnp.int32, (LANES, n_cmp_pad), 0) * SLC_BLOCK
    on = lax.broadcasted_iota(jnp.int32, (LANES, n_cmp_pad), 1) * CMP_STRIDE
    overlap_t = jnp.where((on < oj + SLC_BLOCK) & (on + CMP_LEN > oj), 1.0, 0.0).astype(BF16)
    imp = sum(_dot(overlap_t, part) for part in _split3(p_sum))

    j = lax.broadcasted_iota(jnp.int32, (LANES, tq), 0)
    cur = (t0 + lax.broadcasted_iota(jnp.int32, (LANES, tq), 1)) // SLC_BLOCK
    forced = (j == 0) | (j == cur) | (j == cur - 1)
    cand = jnp.where(jnp.logical_not(forced) & (j <= cur) & (j < n_slc), imp, -1.0)
    sel = jnp.where(forced, 1.0, 0.0)
    for _ in range(max(top_k - 3, 0)):
        best = jnp.max(cand, 0, keepdims=True)
        idx = jnp.min(jnp.where(cand == best, j, LANES), 0, keepdims=True)
        hit = j == idx
        sel = jnp.where(hit & (best >= 0.0), 1.0, sel)
        cand = jnp.where(hit, -2.0, cand)
    bias = jnp.where(sel > 0.0, 0.0, NEG).astype(BF16)
    q_aug = jnp.concatenate([q_t, jnp.concatenate([bias] * NSA_HEADS, axis=1)], axis=0)

    def tile_scores(k_ref, n, q_op, mask):
        k_tile = k_ref[0, pl.ds(pl.multiple_of(n * tq, tq), tq), :]
        s = _dot(k_tile, q_op)
        return (s if mask is None else jnp.where(mask, s, NEG)).astype(BF16)

    def flash_update(m_ref, acc_ref, tiles):
        m_old = m_ref[...]
        m_loc = functools.reduce(jnp.maximum, [jnp.max(s, 0, keepdims=True) for s, _ in tiles])
        m_new = jnp.maximum(m_old, m_loc.astype(F32))
        m_b = m_new.astype(BF16)
        p = jnp.concatenate([jnp.exp2(s - m_b) for s, _ in tiles], axis=0)
        vt = jnp.concatenate([vt_tile for _, vt_tile in tiles], axis=1)
        acc_ref[...] = jnp.exp2(m_old - m_new) * acc_ref[...] + _dot(vt, p)
        m_ref[...] = m_new

    def flash_init(m_ref, acc_ref):
        m_ref[...] = jnp.full(m_ref.shape, NEG, F32)
        acc_ref[...] = jnp.zeros(acc_ref.shape, F32)

    kl = lax.broadcasted_iota(jnp.int32, (tq, hq), 0)
    ql = lane_query((tq, hq))
    causal = kl <= ql

    flash_init(m_sc, acc_sc)
    n_pairs = qi // 2

    def stage_scores(slot, pair):
        for half in range(2):
            s_buf[slot, half] = tile_scores(ks_ref, 2 * pair + half, q_aug, None)

    def consume(slot, pair):
        flash_update(m_sc, acc_sc, [(s_buf[slot, half], vst_ref[0, 2 * pair + half]) for half in range(2)])

    @pl.when(n_pairs > 0)
    def _():
        stage_scores(0, 0)

    def two_pairs(j, carry):
        stage_scores(1, 2 * j + 1)
        consume(0, 2 * j)
        stage_scores(0, jnp.minimum(2 * j + 2, n_pairs - 1))
        consume(1, 2 * j + 1)
        return carry

    lax.fori_loop(0, n_pairs // 2, two_pairs, 0)

    @pl.when(n_pairs % 2 == 1)
    def _():
        consume(0, n_pairs - 1)

    odd = (qi % 2) == 1
    prev = jnp.maximum(qi - 1, 0)
    flash_update(m_sc, acc_sc, [
        (tile_scores(ks_ref, prev, q_aug, jnp.broadcast_to(odd, (tq, hq))), vst_ref[0, prev]),
        (tile_scores(ks_ref, qi, q_aug, causal), vst_ref[0, qi])])

    flash_init(m2_sc, acc2_sc)
    window_tiles = []
    for back in range(0, WINDOW // tq + 1):
        n = jnp.maximum(qi - back, 0)
        if back == 0:
            mask = causal
        elif back * tq == WINDOW:
            mask = (kl > ql) & (qi >= back)
        else:
            mask = jnp.broadcast_to(qi >= back, (tq, hq))
        window_tiles.append((tile_scores(kw_ref, n, q_t, mask), vwt_ref[0, n]))
    flash_update(m2_sc, acc2_sc, window_tiles)

    acc_s, acc_w = acc_sc[...], acc2_sc[...]
    o_s = acc_s[:HEAD_DIM] * (1.0 / acc_s[HEAD_DIM:HEAD_DIM + 1])
    o_w = acc_w[:HEAD_DIM] * (1.0 / acc_w[HEAD_DIM:HEAD_DIM + 1])
    gates_t = _sigmoid(sm_ref[0]).T
    for h in range(NSA_HEADS):
        cols = slice(h * tq, (h + 1) * tq)
        c0 = SM_G + 3 * h
        o_t = (gates_t[c0:c0 + 1, :] * o_c[:, cols] + gates_t[c0 + 1:c0 + 2, :] * o_s[:, cols]
               + gates_t[c0 + 2:c0 + 3, :] * o_w[:, cols])
        o_ref[0, :, h * HEAD_DIM:(h + 1) * HEAD_DIM] = o_t.T.astype(o_ref.dtype)


def _nsa_attention(q_t, small3, k_cmp, v_cmp_t, ks_aug, vs_t, kw_rope, vw_t, tq):
    b, _, s = q_t.shape
    n_cmp_pad = k_cmp.shape[1]
    n_slc = s // SLC_BLOCK
    top_k = min(SLC_TOPK, n_slc)
    vrows = vs_t.shape[2]
    assert WINDOW % tq == 0 and vs_t.shape[-1] == tq
    batch_spec = lambda shape: pl.BlockSpec((1,) + shape, lambda bi, i: (bi,) + (0,) * len(shape))
    hq = NSA_HEADS * tq
    flash_scratch = [pltpu.VMEM((1, hq), F32), pltpu.VMEM((vrows, hq), F32)]
    return pl.pallas_call(
        functools.partial(_nsa_attn_body, tq=tq, n_cmp_pad=n_cmp_pad, n_slc=n_slc, top_k=top_k),
        grid=(b, s // tq),
        in_specs=[
            pl.BlockSpec((1, NSA_Q, tq), lambda bi, i: (bi, 0, i)),
            pl.BlockSpec((1, tq, LANES), lambda bi, i: (bi, i, 0)),
            batch_spec((n_cmp_pad, HEAD_DIM)),
            batch_spec((HEAD_DIM, n_cmp_pad)),
            batch_spec((s, 2 * HEAD_DIM)),
            batch_spec((s // tq, vrows, tq)),
            batch_spec((s, HEAD_DIM)),
            batch_spec((s // tq, vrows, tq)),
        ],
        out_specs=pl.BlockSpec((1, tq, NSA_Q), lambda bi, i: (bi, i, 0)),
        out_shape=jax.ShapeDtypeStruct((b, s, NSA_Q), BF16),
        scratch_shapes=flash_scratch + flash_scratch + [pltpu.VMEM((2, 2, tq, hq), BF16)],
        compiler_params=_cparams("arbitrary", "arbitrary"),
        name="nsa_attn",
    )(q_t, small3, k_cmp, v_cmp_t, ks_aug, vs_t, kw_rope, vw_t)


def _gdn_body(xq_ref, xk_ref, xv_ref, z_ref, sm_ref, cw_ref, hp_ref, nw_ref, o_ref,
              tail_sc, state_sc, qd_sc, kt_sc, u_sc, w_sc, at_sc, eg_sc, *, ct):
    i = pl.program_id(1)
    c = GDN_CHUNK
    nc = ct // c

    @pl.when(i == 0)
    def _():
        tail_sc[...] = jnp.zeros_like(tail_sc)
        state_sc[...] = jnp.zeros_like(state_sc)

    row = lax.broadcasted_iota(jnp.int32, (ct, LANES), 0)
    row_in_chunk = row & (c - 1)
    ii = lax.broadcasted_iota(jnp.int32, (nc, c, c), 1)
    jj = lax.broadcasted_iota(jnp.int32, (nc, c, c), 2)

    def conv_silu(x_ref, part, h):
        col = slice(h * HEAD_DIM, (h + 1) * HEAD_DIM)
        wcol = slice(part * GDN_QK + h * HEAD_DIM, part * GDN_QK + (h + 1) * HEAD_DIM)
        x = x_ref[0, :, col].astype(F32)
        xcat = jnp.concatenate([tail_sc[part, :, col], x], axis=0)
        acc = xcat[8:] * cw_ref[GDN_CONV - 1:GDN_CONV, wcol]
        for tap in range(GDN_CONV - 1):
            shift = GDN_CONV - 1 - tap
            acc = acc + pltpu.roll(xcat, shift, 0)[8:] * cw_ref[tap:tap + 1, wcol]
        tail_sc[part, :, col] = x[ct - 8:]
        return _silu(acc)

    def l2n(v):
        return v * lax.rsqrt(jnp.sum(v * v, -1, keepdims=True) + RMS_EPS)

    sm = sm_ref[0]
    for h in range(GDN_HEADS):
        q = l2n(conv_silu(xq_ref, 0, h)) * (HEAD_DIM ** -0.5)
        k = l2n(conv_silu(xk_ref, 1, h))
        v = conv_silu(xv_ref, 2, h)
        a = sm[:, SM_A + h:SM_A + h + 1]
        bcol = sm[:, SM_B + h:SM_B + h + 1]
        a_log = hp_ref[0:1, h:h + 1]
        dt_bias = hp_ref[1:2, h:h + 1]
        sp_in = a + dt_bias
        softplus = jnp.maximum(sp_in, 0.0) + jnp.log(1.0 + jnp.exp(-jnp.abs(sp_in)))
        g = -jnp.exp(a_log) * softplus
        beta = _sigmoid(bcol)
        gc = jnp.broadcast_to(g, (ct, LANES))
        step = 1
        while step < c:
            gc = gc + jnp.where(row_in_chunk >= step, pltpu.roll(gc, step, 0), 0.0)
            step *= 2
        egc = jnp.exp(gc)
        kb = k * beta
        vb = v * beta
        gc3 = gc.reshape(nc, c, LANES)
        g_col = gc3[:, :, :c]
        g_row = jnp.swapaxes(gc3, 1, 2)[:, :c, :]
        decay = jnp.where(ii >= jj, jnp.exp(jnp.minimum(g_col - g_row, 0.0)), 0.0)
        k3 = k.reshape(nc, c, LANES).astype(BF16)
        kk = jnp.einsum("nid,njd->nij", kb.reshape(nc, c, LANES).astype(BF16), k3, preferred_element_type=F32)
        a_mat = jnp.where(ii > jj, kk * decay, 0.0)
        n_mat = -a_mat
        pw = a_mat
        sq = 2
        while sq < c:
            pwb = pw.astype(BF16)
            pw = jnp.einsum("nij,njk->nik", pwb, pwb, preferred_element_type=F32)
            n_mat = n_mat + pw + jnp.einsum("nij,njk->nik", n_mat.astype(BF16), pw.astype(BF16),
                                            preferred_element_type=F32)
            sq *= 2
        nb = n_mat.astype(BF16)
        vb3 = vb.reshape(nc, c, LANES)
        kbg3 = (kb * egc).reshape(nc, c, LANES)
        u = vb3 + jnp.einsum("nij,njd->nid", nb, vb3.astype(BF16), preferred_element_type=F32)
        w = kbg3 + jnp.einsum("nij,njd->nid", nb, kbg3.astype(BF16), preferred_element_type=F32)
        attn = jnp.einsum("nid,njd->nij", q.reshape(nc, c, LANES).astype(BF16), k3,
                          preferred_element_type=F32) * decay
        g_last = jnp.broadcast_to(gc3[:, c - 1:c, :], (nc, c, LANES))
        qd_sc[h] = (q * egc).astype(BF16)
        kt_sc[h] = (k.reshape(nc, c, LANES) * jnp.exp(g_last - gc3)).reshape(ct, LANES)
        u_sc[h] = u.reshape(ct, LANES)
        w_sc[h] = w.reshape(ct, LANES).astype(BF16)
        at_sc[h] = attn.reshape(ct, c).astype(BF16)
        eg_sc[h] = jnp.exp(g_last).reshape(ct, LANES)

    nw = nw_ref[...]

    def chunk_step(n, carry):
        off = pl.multiple_of(n * c, c)
        rows = pl.ds(off, c)
        for h in range(GDN_HEADS):
            state = state_sc[h]
            sb = state.astype(BF16)
            v_new = u_sc[h, rows, :] - _dot(w_sc[h, rows, :], sb)
            vnb = v_new.astype(BF16)
            o = _dot(qd_sc[h, rows, :], sb) + _dot(at_sc[h, rows, :], vnb)
            decay_last = eg_sc[h, pl.ds(off, 1), :]
            kt_t = kt_sc[h, rows, :].T.astype(BF16)
            state_sc[h] = state * decay_last + _dot(kt_t, vnb)
            zf = z_ref[0, rows, h * HEAD_DIM:(h + 1) * HEAD_DIM].astype(F32)
            o = o * lax.rsqrt(jnp.mean(o * o, -1, keepdims=True) + RMS_EPS) * nw * _silu(zf)
            o_ref[0, rows, h * HEAD_DIM:(h + 1) * HEAD_DIM] = o.astype(o_ref.dtype)
        return carry

    lax.fori_loop(0, nc, chunk_step, 0)


def _gdn(proj3, small3, conv_w, hp, norm_w, ct):
    b, s, _ = proj3.shape
    cb = lambda col: col // GDN_QK
    blk = lambda col: pl.BlockSpec((1, ct, GDN_QK), lambda bi, i: (bi, i, cb(col)))
    hsh = (GDN_HEADS, ct, LANES)
    return pl.pallas_call(
        functools.partial(_gdn_body, ct=ct),
        grid=(b, s // ct),
        in_specs=[blk(COL_GQ), blk(COL_GK), blk(COL_GV), blk(COL_Z),
                  pl.BlockSpec((1, ct, LANES), lambda bi, i: (bi, i, 0)),
                  _resident(conv_w.shape), _resident(hp.shape), _resident(norm_w.shape)],
        out_specs=pl.BlockSpec((1, ct, GDN_V), lambda bi, i: (bi, i, 0)),
        out_shape=jax.ShapeDtypeStruct((b, s, GDN_V), BF16),
        scratch_shapes=[
            pltpu.VMEM((3, 8, GDN_QK), F32),
            pltpu.VMEM((GDN_HEADS, HEAD_DIM, HEAD_DIM), F32),
            pltpu.VMEM(hsh, BF16),
            pltpu.VMEM(hsh, F32),
            pltpu.VMEM(hsh, F32),
            pltpu.VMEM(hsh, BF16),
            pltpu.VMEM((GDN_HEADS, ct, GDN_CHUNK), BF16),
            pltpu.VMEM(hsh, F32),
        ],
        compiler_params=_cparams("arbitrary", "arbitrary"),
        name="gdn",
    )(proj3, proj3, proj3, proj3, small3, conv_w, hp, norm_w)


def _merge_body(x_ref, oa_ref, ob_ref, ga_ref, gb_ref, wa_ref, wb_ref, wm_ref, g_ref, b_ref, o_ref):
    ya = _dot(oa_ref[...], wa_ref[...])
    yb = _dot(ob_ref[...], wb_ref[...])
    y = _sigmoid(ga_ref[...].astype(F32)) * ya + _sigmoid(gb_ref[...].astype(F32)) * yb
    out = _dot(y.astype(BF16), wm_ref[...])
    o_ref[...] = _layer_norm(DN_ALPHA * x_ref[...] + out, g_ref[...], b_ref[...])


def _merge(x2d, oa, ob, proj2d, wa, wb, wm, g, bias, tm):
    t, d = x2d.shape
    row = lambda w, cbi=0: pl.BlockSpec((tm, w), lambda i: (i, cbi))
    return pl.pallas_call(
        _merge_body,
        grid=(t // tm,),
        in_specs=[row(d), row(GDN_V), row(NSA_Q), row(d, COL_GATE_A // d), row(d, COL_GATE_B // d),
                  _resident(wa.shape), _resident(wb.shape), _resident(wm.shape), _resident(g.shape),
                  _resident(bias.shape)],
        out_specs=row(d),
        out_shape=jax.ShapeDtypeStruct((t, d), F32),
        compiler_params=_cparams("arbitrary"),
        name="merge",
    )(x2d, oa, ob, proj2d, proj2d, wa, wb, wm, g, bias)


def _memkv_body(m_ref, wk_ref, wv_ref, k_ref, v_ref):
    m = m_ref[...].astype(BF16)
    k_ref[...] = _dot(m, wk_ref[...]).astype(BF16)
    v_ref[...] = _dot(m, wv_ref[...]).astype(BF16)


def _memkv(mem2d, wk, wv, tm):
    t, d = mem2d.shape
    row = pl.BlockSpec((tm, d), lambda i: (i, 0))
    return pl.pallas_call(
        _memkv_body,
        grid=(t // tm,),
        in_specs=[row, _resident(wk.shape), _resident(wv.shape)],
        out_specs=[row, row],
        out_shape=[jax.ShapeDtypeStruct((t, d), BF16)] * 2,
        compiler_params=_cparams("arbitrary"),
        name="memkv",
    )(mem2d, wk, wv)


def _xattn_body(x_ref, k_ref, v_ref, wq_ref, wo_ref, g_ref, b_ref, o_ref):
    x = x_ref[0]
    q = _dot(x.astype(BF16), wq_ref[...]).astype(BF16)
    heads = []
    for h in range(XA_HEADS):
        sl = slice(h * XA_HEAD_DIM, (h + 1) * XA_HEAD_DIM)
        s = _dot_nt(q[:, sl], k_ref[0, :, sl]) * (XA_HEAD_DIM ** -0.5)
        m = jnp.max(s, -1, keepdims=True)
        e = jnp.exp(s - m)
        p = e * (1.0 / jnp.sum(e, -1, keepdims=True))
        heads.append(_dot(p.astype(BF16), v_ref[0, :, sl]).astype(BF16))
    o = jnp.concatenate(heads, axis=1)
    out = _dot(o, wo_ref[...])
    o_ref[0] = _layer_norm(DN_ALPHA * x + out, g_ref[...], b_ref[...])


def _xattn(x3, k3, v3, wq, wo, g, bias, tm):
    b, s, d = x3.shape
    m = k3.shape[1]
    return pl.pallas_call(
        _xattn_body,
        grid=(b, s // tm),
        in_specs=[pl.BlockSpec((1, tm, d), lambda bi, i: (bi, i, 0)),
                  pl.BlockSpec((1, m, d), lambda bi, i: (bi, 0, 0)),
                  pl.BlockSpec((1, m, d), lambda bi, i: (bi, 0, 0)),
                  _resident(wq.shape), _resident(wo.shape), _resident(g.shape), _resident(bias.shape)],
        out_specs=pl.BlockSpec((1, tm, d), lambda bi, i: (bi, i, 0)),
        out_shape=jax.ShapeDtypeStruct((b, s, d), F32),
        compiler_params=_cparams("arbitrary", "arbitrary"),
        name="xattn",
    )(x3, k3, v3, wq, wo, g, bias)


def _ffn_body(x_ref, wg_ref, wu_ref, wd_ref, g_ref, b_ref, o_ref, *, chunk):
    x = x_ref[...]
    xb = x.astype(BF16)
    out = None
    for c in range(0, wg_ref.shape[1], chunk):
        hid = _silu(_dot(xb, wg_ref[:, c:c + chunk])) * _dot(xb, wu_ref[:, c:c + chunk])
        part = _dot(hid.astype(BF16), wd_ref[c:c + chunk, :])
        out = part if out is None else out + part
    o_ref[...] = _layer_norm(DN_ALPHA * x + out, g_ref[...], b_ref[...])


def _ffn(x2d, wg, wu, wd, g, bias, tm):
    t, d = x2d.shape
    dff = wg.shape[1]
    chunk = dff // 2 if (dff // 2) % LANES == 0 else dff
    row = pl.BlockSpec((tm, d), lambda i: (i, 0))
    return pl.pallas_call(
        functools.partial(_ffn_body, chunk=chunk),
        grid=(t // tm,),
        in_specs=[row, _resident(wg.shape), _resident(wu.shape), _resident(wd.shape), _resident(g.shape),
                  _resident(bias.shape)],
        out_specs=row,
        out_shape=jax.ShapeDtypeStruct((t, d), F32),
        compiler_params=_cparams("arbitrary"),
        name="ffn",
    )(x2d, wg, wu, wd, g, bias)


def _tile(n, pref):
    t = min(n, pref)
    assert n % t == 0, (n, pref)
    return t


def _layer(x, mem, positions, w_in, conv_w, a_log, dt_bias, gdn_norm_w, cmp_pos_k, cmp_pos_v, cmp_k_w1, cmp_k_w2,
           cmp_v_w1, cmp_v_w2, w_up_gdn, w_up_nsa, w_mix_out, ln1_g, ln1_b, xa_wq, xa_wk, xa_wv, xa_wo, ln2_g,
           ln2_b, ffn_w_gate, ffn_w_up, ffn_w_down, ln3_g, ln3_b):
    b, s, d = x.shape
    t = b * s
    assert d == D_MODEL and s % WINDOW == 0 and s // SLC_BLOCK <= LANES and s >= 2 * WINDOW

    o_qkv, o_z = 0, 3 * GDN_QK
    o_a = o_z + GDN_V
    o_b = o_a + GDN_HEADS
    o_qn = o_b + GDN_HEADS
    o_kv6 = o_qn + NSA_Q
    o_gn = o_kv6 + 6 * HEAD_DIM
    o_ga = o_gn + 3 * NSA_HEADS
    o_gb = o_ga + D_MODEL
    w_main = jnp.concatenate([w_in[:, o_ga:o_gb], w_in[:, o_gb:o_gb + D_MODEL], w_in[:, o_qkv:o_z], w_in[:, o_z:o_a],
                              w_in[:, o_qn:o_kv6], w_in[:, o_kv6:o_gn]], axis=1).astype(BF16)
    n_small = 2 * GDN_HEADS + 3 * NSA_HEADS
    w_small = jnp.concatenate([w_in[:, o_a:o_qn], w_in[:, o_gn:o_ga], jnp.zeros((d, LANES - n_small), w_in.dtype)],
                              axis=1).astype(BF16)

    x2d = x.reshape(t, d)
    proj2d, small2d = _project(x2d, w_main, w_small, _tile(t, 512))
    proj3 = proj2d.reshape(b, s, N_MAIN)
    small3 = small2d.reshape(b, s, LANES)

    hp = jnp.zeros((8, LANES), F32).at[0, :GDN_HEADS].set(a_log.astype(F32)).at[1, :GDN_HEADS].set(
        dt_bias.astype(F32))
    o_gdn = _gdn(proj3, small3, conv_w.astype(F32), hp, gdn_norm_w.reshape(1, HEAD_DIM).astype(F32), _tile(s, 512))

    half = HEAD_DIM // 2
    inv = ROPE_THETA ** (-jnp.arange(half, dtype=F32) / half)
    inv_row = jnp.concatenate([inv, inv]).reshape(1, HEAD_DIM)
    posf = positions.astype(F32)
    tq = _tile(s, 256)
    q_t, ks_aug, vs_t, kw_rope, vw_t = _nsa_prep(proj3, posf.reshape(b, s, 1), inv_row, _tile(s, 512), tq)

    n_seg = s // CMP_STRIDE
    seg_w = CMP_STRIDE * HEAD_DIM
    kc_seg = proj3[:, :, COL_KV6:COL_KV6 + HEAD_DIM].reshape(b, n_seg, seg_w)
    vc_seg = proj3[:, :, COL_KV6 + HEAD_DIM:COL_KV6 + 2 * HEAD_DIM].reshape(b, n_seg, seg_w)
    posc = posf[:, CMP_LEN - 1::CMP_STRIDE]
    posc = jnp.concatenate([posc, posc[:, -1:]], axis=1).reshape(b, n_seg, 1)
    k_cmp, v_cmp_t = _compress(kc_seg, vc_seg, cmp_pos_k.reshape(1, CMP_LEN * HEAD_DIM).astype(F32),
                             cmp_pos_v.reshape(1, CMP_LEN * HEAD_DIM).astype(F32), cmp_k_w1.astype(BF16),
                             cmp_k_w2.astype(BF16), cmp_v_w1.astype(BF16), cmp_v_w2.T.astype(BF16), posc, inv_row)
    o_nsa = _nsa_attention(q_t, small3, k_cmp, v_cmp_t, ks_aug, vs_t, kw_rope, vw_t, tq)

    vec = lambda p: p.reshape(1, d).astype(F32)
    x1 = _merge(x2d, o_gdn.reshape(t, GDN_V), o_nsa.reshape(t, NSA_Q), proj2d, w_up_gdn.astype(BF16),
                w_up_nsa.astype(BF16), w_mix_out.astype(BF16), vec(ln1_g), vec(ln1_b), _tile(t, 512))
    mlen = mem.shape[1]
    mk, mv = _memkv(mem.reshape(b * mlen, d), xa_wk.astype(BF16), xa_wv.astype(BF16), _tile(b * mlen, 512))
    x2 = _xattn(x1.reshape(b, s, d), mk.reshape(b, mlen, d), mv.reshape(b, mlen, d), xa_wq.astype(BF16),
                xa_wo.astype(BF16), vec(ln2_g), vec(ln2_b), _tile(s, 512))
    x3 = _ffn(x2.reshape(t, d), ffn_w_gate.astype(BF16), ffn_w_up.astype(BF16), ffn_w_down.astype(BF16),
              vec(ln3_g), vec(ln3_b), _tile(t, 512))
    return x3.reshape(b, s, d)


def kernel(x, mem, positions, w_in, gdn_conv_w, gdn_a_log, gdn_dt_bias, gdn_norm_w, cmp_pos_k, cmp_pos_v, cmp_k_w1, cmp_k_w2, cmp_v_w1, cmp_v_w2, w_up_gdn, w_up_nsa, w_mix_out, ln1_g, ln1_b, xa_wq, xa_wk, xa_wv, xa_wo, ln2_g, ln2_b, ffn_w_gate, ffn_w_up, ffn_w_down, ln3_g, ln3_b):
    for l in range(w_in.shape[0]):
        x = _layer(x, mem, positions, w_in[l], gdn_conv_w[l], gdn_a_log[l], gdn_dt_bias[l], gdn_norm_w[l],
                   cmp_pos_k[l], cmp_pos_v[l], cmp_k_w1[l], cmp_k_w2[l], cmp_v_w1[l], cmp_v_w2[l], w_up_gdn[l],
                   w_up_nsa[l], w_mix_out[l], ln1_g[l], ln1_b[l], xa_wq[l], xa_wk[l], xa_wv[l], xa_wo[l], ln2_g[l],
                   ln2_b[l], ffn_w_gate[l], ffn_w_up[l], ffn_w_down[l], ln3_g[l], ln3_b[l])
    return x
```

```python
import functools

import jax
import jax.numpy as jnp
from jax import lax
from jax.experimental import pallas as pl
from jax.experimental.pallas import tpu as pltpu

F32 = jnp.float32
BF16 = jnp.bfloat16

D_MODEL = 1024
HEAD_DIM = 128
GDN_HEADS = 4
GDN_QK = GDN_HEADS * HEAD_DIM
GDN_V = GDN_HEADS * HEAD_DIM
GDN_CONV = 4
GDN_CHUNK = 64
NSA_HEADS = 4
NSA_Q = NSA_HEADS * HEAD_DIM
CMP_LEN = 32
CMP_STRIDE = 16
SLC_BLOCK = 64
SLC_TOPK = 16
WINDOW = 512
FORCE_SCORE = 1.0e4
XA_HEADS = 4
XA_HEAD_DIM = D_MODEL // XA_HEADS
ROPE_THETA = 10000.0
LN_EPS = 1e-5
RMS_EPS = 1e-6
NEG = -1.0e30
ATTN_SCALE = HEAD_DIM ** -0.5
LOG2E = 1.4426950408889634
DEPTH = 1
DN_ALPHA = (2 * DEPTH) ** 0.25

LANES = 128
VT_ONES_ROWS = 16
VMEM_LIMIT_BYTES = 56 * 1024 * 1024

COL_GATE_A = 0
COL_GATE_B = 1024
COL_GQ = 2048
COL_GK = 2560
COL_GV = 3072
COL_Z = 3584
COL_NQ = 4096
COL_KV6 = 4608
N_MAIN = 5376
SM_A = 0
SM_B = 4
SM_G = 8


def _cparams(*sem):
    return pltpu.CompilerParams(dimension_semantics=sem, vmem_limit_bytes=VMEM_LIMIT_BYTES)


def _resident(shape):
    nd = len(shape)
    return pl.BlockSpec(shape, lambda *_: (0,) * nd, pipeline_mode=pl.Buffered(1))


def _layer_norm(v, g, b):
    mu = jnp.mean(v, -1, keepdims=True)
    c = v - mu
    var = jnp.mean(c * c, -1, keepdims=True)
    return c * lax.rsqrt(var + LN_EPS) * g + b


def _silu(v):
    half = 0.5 * v
    return half + half * jnp.tanh(half)


def _sigmoid(v):
    return 0.5 + 0.5 * jnp.tanh(0.5 * v)


def _dot(a, b):
    return jnp.dot(a, b, preferred_element_type=F32)


def _dot_nt(a, b):
    return lax.dot_general(a, b, (((1,), (1,)), ((), ())), preferred_element_type=F32)


def _proj_body(x_ref, w_ref, ws_ref, o_ref, os_ref, *, chunk):
    x = x_ref[...].astype(BF16)
    for c in range(0, w_ref.shape[1], chunk):
        o_ref[:, c:c + chunk] = _dot(x, w_ref[:, c:c + chunk]).astype(o_ref.dtype)
    os_ref[...] = _dot(x, ws_ref[...])


def _project(x2d, w_main, w_small, tm):
    t, d = x2d.shape
    n = w_main.shape[1]
    return pl.pallas_call(
        functools.partial(_proj_body, chunk=768),
        grid=(t // tm,),
        in_specs=[pl.BlockSpec((tm, d), lambda i: (i, 0)), _resident((d, n)), _resident((d, LANES))],
        out_specs=[pl.BlockSpec((tm, n), lambda i: (i, 0)), pl.BlockSpec((tm, LANES), lambda i: (i, 0))],
        out_shape=[jax.ShapeDtypeStruct((t, n), BF16), jax.ShapeDtypeStruct((t, LANES), F32)],
        compiler_params=_cparams("arbitrary"),
        name="proj",
    )(x2d, w_main, w_small)


def _rope_tables(pos_col, inv_row):
    ang = pos_col * inv_row
    cos = jnp.cos(ang)
    sin = jnp.sin(ang)
    lane = lax.broadcasted_iota(jnp.int32, ang.shape, 1)
    return cos, jnp.where(lane < HEAD_DIM // 2, -sin, sin)


def _rope(v, cos, sin_signed):
    return v * cos + pltpu.roll(v, HEAD_DIM // 2, 1) * sin_signed


def _nsa_prep_body(pos_ref, inv_ref, q_ref, ks_ref, vs_ref, kw_ref, vw_ref,
                   qt_ref, kso_ref, vst_ref, kwo_ref, vwt_ref, *, ct, tk):
    i = pl.program_id(1)
    cos, sin_s = _rope_tables(pos_ref[0], inv_ref[...])
    for h in range(NSA_HEADS):
        sl = slice(h * HEAD_DIM, (h + 1) * HEAD_DIM)
        q_rot = _rope(q_ref[0, :, sl].astype(F32), cos, sin_s) * (ATTN_SCALE * LOG2E)
        qt_ref[0, sl, :] = q_rot.T.astype(BF16)
    kso_ref[0, :, :HEAD_DIM] = _rope(ks_ref[0].astype(F32), cos, sin_s).astype(BF16)
    lane = lax.broadcasted_iota(jnp.int32, (ct, LANES), 1)
    tok = lax.broadcasted_iota(jnp.int32, (ct, LANES), 0) + i * ct
    kso_ref[0, :, HEAD_DIM:] = jnp.where(lane == tok // SLC_BLOCK, 1.0, 0.0).astype(BF16)
    kwo_ref[0] = _rope(kw_ref[0].astype(F32), cos, sin_s).astype(BF16)
    ones = jnp.ones((VT_ONES_ROWS, tk), BF16)
    for v_ref, vt_ref in ((vs_ref, vst_ref), (vw_ref, vwt_ref)):
        v_t = v_ref[0].astype(F32).T
        for n in range(ct // tk):
            vt_ref[0, n, :HEAD_DIM, :] = v_t[:, n * tk:(n + 1) * tk].astype(BF16)
            vt_ref[0, n, HEAD_DIM:, :] = ones


def _nsa_prep(proj3, posf, inv_row, ct, tk):
    b, s, _ = proj3.shape
    kv_col = COL_KV6 // HEAD_DIM
    col = lambda c: pl.BlockSpec((1, ct, HEAD_DIM), lambda bi, i: (bi, i, c))
    vrows = HEAD_DIM + VT_ONES_ROWS
    vt_spec = pl.BlockSpec((1, ct // tk, vrows, tk), lambda bi, i: (bi, i, 0, 0))
    vt_shape = jax.ShapeDtypeStruct((b, s // tk, vrows, tk), BF16)
    return pl.pallas_call(
        functools.partial(_nsa_prep_body, ct=ct, tk=tk),
        grid=(b, s // ct),
        in_specs=[
            pl.BlockSpec((1, ct, 1), lambda bi, i: (bi, i, 0)),
            _resident((1, LANES)),
            pl.BlockSpec((1, ct, NSA_Q), lambda bi, i: (bi, i, COL_NQ // NSA_Q)),
            col(kv_col + 2), col(kv_col + 3), col(kv_col + 4), col(kv_col + 5),
        ],
        out_specs=[
            pl.BlockSpec((1, NSA_Q, ct), lambda bi, i: (bi, 0, i)),
            pl.BlockSpec((1, ct, 2 * HEAD_DIM), lambda bi, i: (bi, i, 0)),
            vt_spec,
            pl.BlockSpec((1, ct, HEAD_DIM), lambda bi, i: (bi, i, 0)),
            vt_spec,
        ],
        out_shape=[
            jax.ShapeDtypeStruct((b, NSA_Q, s), BF16),
            jax.ShapeDtypeStruct((b, s, 2 * HEAD_DIM), BF16),
            vt_shape,
            jax.ShapeDtypeStruct((b, s, HEAD_DIM), BF16),
            vt_shape,
        ],
        compiler_params=_cparams("arbitrary", "arbitrary"),
        name="nsa_prep",
    )(posf, inv_row, proj3, proj3, proj3, proj3, proj3)


def _compress_body(kc_ref, vc_ref, pk_ref, pv_ref, kw1_ref, kw2_ref, vw1_ref, vw2_ref, posc_ref, inv_ref,
                   ko_ref, vo_ref):
    def comp(seg_ref, p_ref, w1_ref, w2_ref, transposed):
        seg = seg_ref[0]
        n_seg, half = seg.shape
        first = _dot(seg, w1_ref[:half, :])
        second = _dot(seg, w1_ref[half:, :])
        pos_term = _dot(jnp.broadcast_to(p_ref[...], (8, p_ref.shape[1])).astype(BF16), w1_ref[...])[0:1]
        hidden = first + pltpu.roll(second, n_seg - 1, 0) + pos_term
        act = _silu(hidden)
        if transposed:
            return _dot(w2_ref[...], act.T.astype(BF16))
        return _dot(act.astype(BF16), w2_ref[...])

    kc = comp(kc_ref, pk_ref, kw1_ref, kw2_ref, False)
    cos, sin_s = _rope_tables(posc_ref[0], inv_ref[...])
    ko_ref[0] = _rope(kc, cos, sin_s).astype(BF16)
    vo_ref[0] = comp(vc_ref, pv_ref, vw1_ref, vw2_ref, True).astype(BF16)


def _compress(kc_seg, vc_seg, pos_k, pos_v, kw1, kw2, vw1, vw2, posc, inv_row):
    b, n_seg, seg_w = kc_seg.shape
    seg_spec = pl.BlockSpec((1, n_seg, seg_w), lambda bi: (bi, 0, 0))
    return pl.pallas_call(
        _compress_body,
        grid=(b,),
        in_specs=[seg_spec, seg_spec, _resident(pos_k.shape), _resident(pos_v.shape), _resident(kw1.shape),
                  _resident(kw2.shape), _resident(vw1.shape), _resident(vw2.shape),
                  pl.BlockSpec((1, n_seg, 1), lambda bi: (bi, 0, 0)), _resident((1, LANES))],
        out_specs=[pl.BlockSpec((1, n_seg, HEAD_DIM), lambda bi: (bi, 0, 0)),
                   pl.BlockSpec((1, HEAD_DIM, n_seg), lambda bi: (bi, 0, 0))],
        out_shape=[jax.ShapeDtypeStruct((b, n_seg, HEAD_DIM), BF16),
                   jax.ShapeDtypeStruct((b, HEAD_DIM, n_seg), BF16)],
        compiler_params=_cparams("arbitrary"),
        name="compress",
    )(kc_seg, vc_seg, pos_k, pos_v, kw1, kw2, vw1, vw2, posc, inv_row)


def _split3(v):
    hi = v.astype(BF16)
    r1 = v - hi.astype(F32)
    mid = r1.astype(BF16)
    lo = (r1 - mid.astype(F32)).astype(BF16)
    return hi, mid, lo


def _nsa_attn_body(qt_ref, sm_ref, kc_ref, vct_ref, ks_ref, vst_ref, kw_ref, vwt_ref, o_ref,
                   m_sc, acc_sc, m2_sc, acc2_sc, s_buf, *, tq, n_cmp_pad, n_slc, top_k):
    qi = pl.program_id(1)
    t0 = qi * tq
    hq = NSA_HEADS * tq
    q_t = jnp.concatenate([qt_ref[0, h * HEAD_DIM:(h + 1) * HEAD_DIM, :] for h in range(NSA_HEADS)], axis=1)

    def lane_query(shape):
        return lax.broadcasted_iota(jnp.int32, shape, 1) & (tq - 1)

    s_c = _dot(kc_ref[0], q_t)
    n_id = lax.broadcasted_iota(jnp.int32, (n_cmp_pad, hq), 0)
    valid_c = (n_id * CMP_STRIDE + (CMP_LEN - 1)) <= t0 + lane_query((n_cmp_pad, hq))
    m_c = jnp.max(jnp.where(valid_c, s_c, NEG), 0, keepdims=True)
    e_c = jnp.where(valid_c, jnp.exp2(s_c - m_c), 0.0)
    l_c = jnp.sum(e_c, 0, keepdims=True)
    p_c = e_c * (1.0 / jnp.where(l_c > 0.0, l_c, 1.0))
    o_c = _dot(vct_ref[0], p_c.astype(BF16))

    p_sum = p_c[:, 0:tq]
    for h in range(1, NSA_HEADS):
        p_sum = p_sum + p_c[:, h * tq:(h + 1) * tq]
    oj = lax.broadcasted_iota(jnp.int32, (LANES, n_cmp_pad), 0) * SLC_BLOCK
    on = lax.broadcasted_iota(jnp.int32, (LANES, n_cmp_pad), 1) * CMP_STRIDE
    overlap_t = jnp.where((on < oj + SLC_BLOCK) & (on + CMP_LEN > oj), 1.0, 0.0).astype(BF16)
    imp = sum(_dot(overlap_t, part) for part in _split3(p_sum))

    j = lax.broadcasted_iota(jnp.int32, (LANES, tq), 0)
    cur = (t0 + lax.broadcasted_iota(jnp.int32, (LANES, tq), 1)) // SLC_BLOCK
    forced = (j == 0) | (j == cur) | (j == cur - 1)
    cand = jnp.where(jnp.logical_not(forced) & (j <= cur) & (j < n_slc), imp, -1.0)
    sel = jnp.where(forced, 1.0, 0.0)
    for _ in range(max(top_k - 3, 0)):
        best = jnp.max(cand, 0, keepdims=True)
        idx = jnp.min(jnp.where(cand == best, j, LANES), 0, keepdims=True)
        hit = j == idx
        sel = jnp.where(hit & (best >= 0.0), 1.0, sel)
        cand = jnp.where(hit, -2.0, cand)
    bias = jnp.where(sel > 0.0, 0.0, NEG).astype(BF16)
    q_aug = jnp.concatenate([q_t, jnp.concatenate([bias] * NSA_HEADS, axis=1)], axis=0)

    def tile_scores(k_ref, n, q_op, mask):
        k_tile = k_ref[0, pl.ds(pl.multiple_of(n * tq, tq), tq), :]
        s = _dot(k_tile, q_op)
        return (s if mask is None else jnp.where(mask, s, NEG)).astype(BF16)

    def flash_update(m_ref, acc_ref, tiles):
        m_old = m_ref[...]
        m_loc = functools.reduce(jnp.maximum, [jnp.max(s, 0, keepdims=True) for s, _ in tiles])
        m_new = jnp.maximum(m_old, m_loc.astype(F32))
        m_b = m_new.astype(BF16)
        p = jnp.concatenate([jnp.exp2(s - m_b) for s, _ in tiles], axis=0)
        vt = jnp.concatenate([vt_tile for _, vt_tile in tiles], axis=1)
        acc_ref[...] = jnp.exp2(m_old - m_new) * acc_ref[...] + _dot(vt, p)
        m_ref[...] = m_new

    def flash_init(m_ref, acc_ref):
        m_ref[...] = jnp.full(m_ref.shape, NEG, F32)
        acc_ref[...] = jnp.zeros(acc_ref.shape, F32)

    kl = lax.broadcasted_iota(jnp.int32, (tq, hq), 0)
    ql = lane_query((tq, hq))
    causal = kl <= ql

    flash_init(m_sc, acc_sc)
    n_pairs = qi // 2

    def stage_scores(slot, pair):
        for half in range(2):
            s_buf[slot, half] = tile_scores(ks_ref, 2 * pair + half, q_aug, None)

    def consume(slot, pair):
        flash_update(m_sc, acc_sc, [(s_buf[slot, half], vst_ref[0, 2 * pair + half]) for half in range(2)])

    @pl.when(n_pairs > 0)
    def _():
        stage_scores(0, 0)

    def two_pairs(j, carry):
        stage_scores(1, 2 * j + 1)
        consume(0, 2 * j)
        stage_scores(0, jnp.minimum(2 * j + 2, n_pairs - 1))
        consume(1, 2 * j + 1)
        return carry

    lax.fori_loop(0, n_pairs // 2, two_pairs, 0)

    @pl.when(n_pairs % 2 == 1)
    def _():
        consume(0, n_pairs - 1)

    odd = (qi % 2) == 1
    prev = jnp.maximum(qi - 1, 0)
    flash_update(m_sc, acc_sc, [
        (tile_scores(ks_ref, prev, q_aug, jnp.broadcast_to(odd, (tq, hq))), vst_ref[0, prev]),
        (tile_scores(ks_ref, qi, q_aug, causal), vst_ref[0, qi])])

    flash_init(m2_sc, acc2_sc)
    window_tiles = []
    for back in range(0, WINDOW // tq + 1):
        n = jnp.maximum(qi - back, 0)
        if back == 0:
            mask = causal
        elif back * tq == WINDOW:
            mask = (kl > ql) & (qi >= back)
        else:
            mask = jnp.broadcast_to(qi >= back, (tq, hq))
        window_tiles.append((tile_scores(kw_ref, n, q_t, mask), vwt_ref[0, n]))
    flash_update(m2_sc, acc2_sc, window_tiles)

    acc_s, acc_w = acc_sc[...], acc2_sc[...]
    o_s = acc_s[:HEAD_DIM] * (1.0 / acc_s[HEAD_DIM:HEAD_DIM + 1])
    o_w = acc_w[:HEAD_DIM] * (1.0 / acc_w[HEAD_DIM:HEAD_DIM + 1])
    gates_t = _sigmoid(sm_ref[0]).T
    eye = jnp.where(lax.broadcasted_iota(jnp.int32, (tq, tq), 0) == lax.broadcasted_iota(jnp.int32, (tq, tq), 1),
                    1.0, 0.0).astype(BF16)
    for h in range(NSA_HEADS):
        cols = slice(h * tq, (h + 1) * tq)
        c0 = SM_G + 3 * h
        o_t = (gates_t[c0:c0 + 1, :] * o_c[:, cols] + gates_t[c0 + 1:c0 + 2, :] * o_s[:, cols]
               + gates_t[c0 + 2:c0 + 3, :] * o_w[:, cols])
        o_ref[0, :, h * HEAD_DIM:(h + 1) * HEAD_DIM] = _dot_nt(eye, o_t.astype(BF16)).astype(o_ref.dtype)


def _nsa_attention(q_t, small3, k_cmp, v_cmp_t, ks_aug, vs_t, kw_rope, vw_t, tq):
    b, _, s = q_t.shape
    n_cmp_pad = k_cmp.shape[1]
    n_slc = s // SLC_BLOCK
    top_k = min(SLC_TOPK, n_slc)
    vrows = vs_t.shape[2]
    assert WINDOW % tq == 0 and vs_t.shape[-1] == tq
    batch_spec = lambda shape: pl.BlockSpec((1,) + shape, lambda bi, i: (bi,) + (0,) * len(shape))
    hq = NSA_HEADS * tq
    flash_scratch = [pltpu.VMEM((1, hq), F32), pltpu.VMEM((vrows, hq), F32)]
    return pl.pallas_call(
        functools.partial(_nsa_attn_body, tq=tq, n_cmp_pad=n_cmp_pad, n_slc=n_slc, top_k=top_k),
        grid=(b, s // tq),
        in_specs=[
            pl.BlockSpec((1, NSA_Q, tq), lambda bi, i: (bi, 0, i)),
            pl.BlockSpec((1, tq, LANES), lambda bi, i: (bi, i, 0)),
            batch_spec((n_cmp_pad, HEAD_DIM)),
            batch_spec((HEAD_DIM, n_cmp_pad)),
            batch_spec((s, 2 * HEAD_DIM)),
            batch_spec((s // tq, vrows, tq)),
            batch_spec((s, HEAD_DIM)),
            batch_spec((s // tq, vrows, tq)),
        ],
        out_specs=pl.BlockSpec((1, tq, NSA_Q), lambda bi, i: (bi, i, 0)),
        out_shape=jax.ShapeDtypeStruct((b, s, NSA_Q), BF16),
        scratch_shapes=flash_scratch + flash_scratch + [pltpu.VMEM((2, 2, tq, hq), BF16)],
        compiler_params=_cparams("arbitrary", "arbitrary"),
        name="nsa_attn",
    )(q_t, small3, k_cmp, v_cmp_t, ks_aug, vs_t, kw_rope, vw_t)


def _gdn_body(xq_ref, xk_ref, xv_ref, z_ref, sm_ref, cw_ref, hp_ref, nw_ref, o_ref,
              xpad_sc, state_sc, qp_sc, op_sc, mp_sc, nn_sc, eg_sc, *, ct):
    i = pl.program_id(1)
    c = GDN_CHUNK
    nc = ct // c

    @pl.when(i == 0)
    def _():
        xpad_sc[:, 0:8, :] = jnp.zeros((3, 8, GDN_QK), F32)
        state_sc[...] = jnp.zeros_like(state_sc)

    row = lax.broadcasted_iota(jnp.int32, (ct, LANES), 0)
    row_in_chunk = row & (c - 1)
    ii = lax.broadcasted_iota(jnp.int32, (nc, c, c), 1)
    jj = lax.broadcasted_iota(jnp.int32, (nc, c, c), 2)

    def conv_silu(x_ref, part, h):
        col = slice(h * HEAD_DIM, (h + 1) * HEAD_DIM)
        wcol = slice(part * GDN_QK + h * HEAD_DIM, part * GDN_QK + (h + 1) * HEAD_DIM)
        xpad_sc[part, 8:, col] = x_ref[0, :, col].astype(F32)
        acc = None
        for tap in range(GDN_CONV):
            term = xpad_sc[part, 8 - (GDN_CONV - 1 - tap):8 - (GDN_CONV - 1 - tap) + ct, col] * cw_ref[tap:tap + 1, wcol]
            acc = term if acc is None else acc + term
        xpad_sc[part, 0:8, col] = xpad_sc[part, ct:ct + 8, col]
        return _silu(acc)

    def l2n(v):
        return v * lax.rsqrt(jnp.sum(v * v, -1, keepdims=True) + RMS_EPS)

    sm = sm_ref[0]
    sp_in = sm + hp_ref[1:2, :]
    softplus = jnp.maximum(sp_in, 0.0) + jnp.log(1.0 + jnp.exp(-jnp.abs(sp_in)))
    g_all = -jnp.exp(hp_ref[0:1, :]) * softplus
    beta_all = _sigmoid(sm)
    for h in range(GDN_HEADS):
        q = l2n(conv_silu(xq_ref, 0, h)) * (HEAD_DIM ** -0.5)
        k = l2n(conv_silu(xk_ref, 1, h))
        v = conv_silu(xv_ref, 2, h)
        g = g_all[:, SM_A + h:SM_A + h + 1]
        beta = beta_all[:, SM_B + h:SM_B + h + 1]
        gc = jnp.broadcast_to(g, (ct, LANES))
        step = 1
        while step < c:
            gc = gc + jnp.where(row_in_chunk >= step, pltpu.roll(gc, step, 0), 0.0)
            step *= 2
        egc = jnp.exp(gc)
        kb = k * beta
        vb = v * beta
        gc3 = gc.reshape(nc, c, LANES)
        g_col = gc3[:, :, :c]
        g_row = jnp.swapaxes(gc3, 1, 2)[:, :c, :]
        decay = jnp.where(ii >= jj, jnp.exp(jnp.minimum(g_col - g_row, 0.0)), 0.0)
        k3 = k.reshape(nc, c, LANES).astype(BF16)
        kk = jnp.einsum("nid,njd->nij", kb.reshape(nc, c, LANES).astype(BF16), k3, preferred_element_type=F32)
        a_mat = jnp.where(ii > jj, kk * decay, 0.0)
        n_mat = -a_mat
        pw = a_mat
        sq = 2
        while sq < c:
            pwb = pw.astype(BF16)
            pw = jnp.einsum("nij,njk->nik", pwb, pwb, preferred_element_type=F32)
            n_mat = n_mat + pw + jnp.einsum("nij,njk->nik", n_mat.astype(BF16), pw.astype(BF16),
                                            preferred_element_type=F32)
            sq *= 2
        nb = n_mat.astype(BF16)
        vb3 = vb.reshape(nc, c, LANES)
        kbg3 = (kb * egc).reshape(nc, c, LANES)
        u = vb3 + jnp.einsum("nij,njd->nid", nb, vb3.astype(BF16), preferred_element_type=F32)
        w = kbg3 + jnp.einsum("nij,njd->nid", nb, kbg3.astype(BF16), preferred_element_type=F32)
        attn = jnp.einsum("nid,njd->nij", q.reshape(nc, c, LANES).astype(BF16), k3,
                          preferred_element_type=F32) * decay
        g_last = gc3[:, c - 1:c, :]
        wb = w.astype(BF16)
        ub = u.astype(BF16)
        ab = attn.astype(BF16)
        kt_t = jnp.swapaxes(k.reshape(nc, c, LANES) * jnp.exp(g_last - gc3), 1, 2).astype(BF16)
        qd3 = (q * egc).reshape(nc, c, LANES)
        qp = qd3 - jnp.einsum("nij,njd->nid", ab, wb, preferred_element_type=F32)
        op = jnp.einsum("nij,nje->nie", ab, ub, preferred_element_type=F32)
        qp_sc[h] = qp.reshape(ct, LANES).astype(BF16)
        op_sc[h] = op.reshape(ct, LANES)
        mp_sc[h] = jnp.einsum("ndi,nie->nde", kt_t, wb, preferred_element_type=F32).astype(BF16)
        nn_sc[h] = jnp.einsum("ndi,nie->nde", kt_t, ub, preferred_element_type=F32)
        eg_sc[h] = jnp.broadcast_to(jnp.exp(g_last), (nc, 8, LANES))

    nw = nw_ref[...]

    def chunk_step(n, carry):
        rows = pl.ds(pl.multiple_of(n * c, c), c)
        for h in range(GDN_HEADS):
            state = state_sc[h]
            sb = state.astype(BF16)
            o = _dot(qp_sc[h, rows, :], sb) + op_sc[h, rows, :]
            chunk_decay = eg_sc[h, n][0:1]
            state_sc[h] = state * chunk_decay - _dot(mp_sc[h, n], sb) + nn_sc[h, n]
            zf = z_ref[0, rows, h * HEAD_DIM:(h + 1) * HEAD_DIM].astype(F32)
            o = o * lax.rsqrt(jnp.mean(o * o, -1, keepdims=True) + RMS_EPS) * nw * _silu(zf)
            o_ref[0, rows, h * HEAD_DIM:(h + 1) * HEAD_DIM] = o.astype(o_ref.dtype)
        return carry

    lax.fori_loop(0, nc, chunk_step, 0)


def _gdn(proj3, small3, conv_w, hp, norm_w, ct):
    b, s, _ = proj3.shape
    cb = lambda col: col // GDN_QK
    blk = lambda col: pl.BlockSpec((1, ct, GDN_QK), lambda bi, i: (bi, i, cb(col)))
    hsh = (GDN_HEADS, ct, LANES)
    nc = ct // GDN_CHUNK
    return pl.pallas_call(
        functools.partial(_gdn_body, ct=ct),
        grid=(b, s // ct),
        in_specs=[blk(COL_GQ), blk(COL_GK), blk(COL_GV), blk(COL_Z),
                  pl.BlockSpec((1, ct, LANES), lambda bi, i: (bi, i, 0)),
                  _resident(conv_w.shape), _resident(hp.shape), _resident(norm_w.shape)],
        out_specs=pl.BlockSpec((1, ct, GDN_V), lambda bi, i: (bi, i, 0)),
        out_shape=jax.ShapeDtypeStruct((b, s, GDN_V), BF16),
        scratch_shapes=[
            pltpu.VMEM((3, ct + 8, GDN_QK), F32),
            pltpu.VMEM((GDN_HEADS, HEAD_DIM, HEAD_DIM), F32),
            pltpu.VMEM(hsh, BF16),
            pltpu.VMEM(hsh, F32),
            pltpu.VMEM((GDN_HEADS, nc, HEAD_DIM, HEAD_DIM), BF16),
            pltpu.VMEM((GDN_HEADS, nc, HEAD_DIM, HEAD_DIM), F32),
            pltpu.VMEM((GDN_HEADS, nc, 8, LANES), F32),
        ],
        compiler_params=_cparams("arbitrary", "arbitrary"),
        name="gdn",
    )(proj3, proj3, proj3, proj3, small3, conv_w, hp, norm_w)


def _merge_body(x_ref, oa_ref, ob_ref, ga_ref, gb_ref, wa_ref, wb_ref, wm_ref, g_ref, b_ref, o_ref):
    ya = _dot(oa_ref[...], wa_ref[...])
    yb = _dot(ob_ref[...], wb_ref[...])
    y = _sigmoid(ga_ref[...].astype(F32)) * ya + _sigmoid(gb_ref[...].astype(F32)) * yb
    out = _dot(y.astype(BF16), wm_ref[...])
    o_ref[...] = _layer_norm(DN_ALPHA * x_ref[...] + out, g_ref[...], b_ref[...])


def _merge(x2d, oa, ob, proj2d, wa, wb, wm, g, bias, tm):
    t, d = x2d.shape
    row = lambda w, cbi=0: pl.BlockSpec((tm, w), lambda i: (i, cbi))
    return pl.pallas_call(
        _merge_body,
        grid=(t // tm,),
        in_specs=[row(d), row(GDN_V), row(NSA_Q), row(d, COL_GATE_A // d), row(d, COL_GATE_B // d),
                  _resident(wa.shape), _resident(wb.shape), _resident(wm.shape), _resident(g.shape),
                  _resident(bias.shape)],
        out_specs=row(d),
        out_shape=jax.ShapeDtypeStruct((t, d), F32),
        compiler_params=_cparams("arbitrary"),
        name="merge",
    )(x2d, oa, ob, proj2d, proj2d, wa, wb, wm, g, bias)


def _memkv_body(m_ref, wk_ref, wv_ref, k_ref, v_ref):
    m = m_ref[...].astype(BF16)
    k_ref[...] = _dot(m, wk_ref[...]).astype(BF16)
    v_ref[...] = _dot(m, wv_ref[...]).astype(BF16)


def _memkv(mem2d, wk, wv, tm):
    t, d = mem2d.shape
    row = pl.BlockSpec((tm, d), lambda i: (i, 0))
    return pl.pallas_call(
        _memkv_body,
        grid=(t // tm,),
        in_specs=[row, _resident(wk.shape), _resident(wv.shape)],
        out_specs=[row, row],
        out_shape=[jax.ShapeDtypeStruct((t, d), BF16)] * 2,
        compiler_params=_cparams("arbitrary"),
        name="memkv",
    )(mem2d, wk, wv)


def _xattn_body(x_ref, k_ref, v_ref, wq_ref, wo_ref, g_ref, b_ref, o_ref):
    x = x_ref[0]
    q = _dot(x.astype(BF16), wq_ref[...]).astype(BF16)
    heads = []
    for h in range(XA_HEADS):
        sl = slice(h * XA_HEAD_DIM, (h + 1) * XA_HEAD_DIM)
        s = _dot_nt(q[:, sl], k_ref[0, :, sl]) * (XA_HEAD_DIM ** -0.5)
        m = jnp.max(s, -1, keepdims=True)
        e = jnp.exp(s - m)
        p = e * (1.0 / jnp.sum(e, -1, keepdims=True))
        heads.append(_dot(p.astype(BF16), v_ref[0, :, sl]).astype(BF16))
    o = jnp.concatenate(heads, axis=1)
    out = _dot(o, wo_ref[...])
    o_ref[0] = _layer_norm(DN_ALPHA * x + out, g_ref[...], b_ref[...])


def _xattn(x3, k3, v3, wq, wo, g, bias, tm):
    b, s, d = x3.shape
    m = k3.shape[1]
    return pl.pallas_call(
        _xattn_body,
        grid=(b, s // tm),
        in_specs=[pl.BlockSpec((1, tm, d), lambda bi, i: (bi, i, 0)),
                  pl.BlockSpec((1, m, d), lambda bi, i: (bi, 0, 0)),
                  pl.BlockSpec((1, m, d), lambda bi, i: (bi, 0, 0)),
                  _resident(wq.shape), _resident(wo.shape), _resident(g.shape), _resident(bias.shape)],
        out_specs=pl.BlockSpec((1, tm, d), lambda bi, i: (bi, i, 0)),
        out_shape=jax.ShapeDtypeStruct((b, s, d), F32),
        compiler_params=_cparams("arbitrary", "arbitrary"),
        name="xattn",
    )(x3, k3, v3, wq, wo, g, bias)


def _ffn_body(x_ref, wg_ref, wu_ref, wd_ref, g_ref, b_ref, o_ref, *, chunk):
    x = x_ref[...]
    xb = x.astype(BF16)
    out = None
    for c in range(0, wg_ref.shape[1], chunk):
        hid = _silu(_dot(xb, wg_ref[:, c:c + chunk])) * _dot(xb, wu_ref[:, c:c + chunk])
        part = _dot(hid.astype(BF16), wd_ref[c:c + chunk, :])
        out = part if out is None else out + part
    o_ref[...] = _layer_norm(DN_ALPHA * x + out, g_ref[...], b_ref[...])


def _ffn(x2d, wg, wu, wd, g, bias, tm):
    t, d = x2d.shape
    dff = wg.shape[1]
    chunk = dff // 2 if (dff // 2) % LANES == 0 else dff
    row = pl.BlockSpec((tm, d), lambda i: (i, 0))
    return pl.pallas_call(
        functools.partial(_ffn_body, chunk=chunk),
        grid=(t // tm,),
        in_specs=[row, _resident(wg.shape), _resident(wu.shape), _resident(wd.shape), _resident(g.shape),
                  _resident(bias.shape)],
        out_specs=row,
        out_shape=jax.ShapeDtypeStruct((t, d), F32),
        compiler_params=_cparams("arbitrary"),
        name="ffn",
    )(x2d, wg, wu, wd, g, bias)


def _tile(n, pref):
    t = min(n, pref)
    assert n % t == 0, (n, pref)
    return t


def _layer(x, mem, positions, w_in, conv_w, a_log, dt_bias, gdn_norm_w, cmp_pos_k, cmp_pos_v, cmp_k_w1, cmp_k_w2,
           cmp_v_w1, cmp_v_w2, w_up_gdn, w_up_nsa, w_mix_out, ln1_g, ln1_b, xa_wq, xa_wk, xa_wv, xa_wo, ln2_g,
           ln2_b, ffn_w_gate, ffn_w_up, ffn_w_down, ln3_g, ln3_b):
    b, s, d = x.shape
    t = b * s
    assert d == D_MODEL and s % WINDOW == 0 and s // SLC_BLOCK <= LANES and s >= 2 * WINDOW

    o_qkv, o_z = 0, 3 * GDN_QK
    o_a = o_z + GDN_V
    o_b = o_a + GDN_HEADS
    o_qn = o_b + GDN_HEADS
    o_kv6 = o_qn + NSA_Q
    o_gn = o_kv6 + 6 * HEAD_DIM
    o_ga = o_gn + 3 * NSA_HEADS
    o_gb = o_ga + D_MODEL
    w_main = jnp.concatenate([w_in[:, o_ga:o_gb], w_in[:, o_gb:o_gb + D_MODEL], w_in[:, o_qkv:o_z], w_in[:, o_z:o_a],
                              w_in[:, o_qn:o_kv6], w_in[:, o_kv6:o_gn]], axis=1).astype(BF16)
    n_small = 2 * GDN_HEADS + 3 * NSA_HEADS
    w_small = jnp.concatenate([w_in[:, o_a:o_qn], w_in[:, o_gn:o_ga], jnp.zeros((d, LANES - n_small), w_in.dtype)],
                              axis=1).astype(BF16)

    x2d = x.reshape(t, d)
    proj2d, small2d = _project(x2d, w_main, w_small, _tile(t, 512))
    proj3 = proj2d.reshape(b, s, N_MAIN)
    small3 = small2d.reshape(b, s, LANES)

    hp = jnp.zeros((8, LANES), F32).at[0, :GDN_HEADS].set(a_log.astype(F32)).at[1, :GDN_HEADS].set(
        dt_bias.astype(F32))
    o_gdn = _gdn(proj3, small3, conv_w.astype(F32), hp, gdn_norm_w.reshape(1, HEAD_DIM).astype(F32), _tile(s, 512))

    half = HEAD_DIM // 2
    inv = ROPE_THETA ** (-jnp.arange(half, dtype=F32) / half)
    inv_row = jnp.concatenate([inv, inv]).reshape(1, HEAD_DIM)
    posf = positions.astype(F32)
    tq = _tile(s, 256)
    q_t, ks_aug, vs_t, kw_rope, vw_t = _nsa_prep(proj3, posf.reshape(b, s, 1), inv_row, _tile(s, 512), tq)

    n_seg = s // CMP_STRIDE
    seg_w = CMP_STRIDE * HEAD_DIM
    kc_seg = proj3[:, :, COL_KV6:COL_KV6 + HEAD_DIM].reshape(b, n_seg, seg_w)
    vc_seg = proj3[:, :, COL_KV6 + HEAD_DIM:COL_KV6 + 2 * HEAD_DIM].reshape(b, n_seg, seg_w)
    posc = posf[:, CMP_LEN - 1::CMP_STRIDE]
    posc = jnp.concatenate([posc, posc[:, -1:]], axis=1).reshape(b, n_seg, 1)
    k_cmp, v_cmp_t = _compress(kc_seg, vc_seg, cmp_pos_k.reshape(1, CMP_LEN * HEAD_DIM).astype(F32),
                             cmp_pos_v.reshape(1, CMP_LEN * HEAD_DIM).astype(F32), cmp_k_w1.astype(BF16),
                             cmp_k_w2.astype(BF16), cmp_v_w1.astype(BF16), cmp_v_w2.T.astype(BF16), posc, inv_row)
    o_nsa = _nsa_attention(q_t, small3, k_cmp, v_cmp_t, ks_aug, vs_t, kw_rope, vw_t, tq)

    vec = lambda p: p.reshape(1, d).astype(F32)
    x1 = _merge(x2d, o_gdn.reshape(t, GDN_V), o_nsa.reshape(t, NSA_Q), proj2d, w_up_gdn.astype(BF16),
                w_up_nsa.astype(BF16), w_mix_out.astype(BF16), vec(ln1_g), vec(ln1_b), _tile(t, 512))
    mlen = mem.shape[1]
    mk, mv = _memkv(mem.reshape(b * mlen, d), xa_wk.astype(BF16), xa_wv.astype(BF16), _tile(b * mlen, 512))
    x2 = _xattn(x1.reshape(b, s, d), mk.reshape(b, mlen, d), mv.reshape(b, mlen, d), xa_wq.astype(BF16),
                xa_wo.astype(BF16), vec(ln2_g), vec(ln2_b), _tile(s, 512))
    x3 = _ffn(x2.reshape(t, d), ffn_w_gate.astype(BF16), ffn_w_up.astype(BF16), ffn_w_down.astype(BF16),
              vec(ln3_g), vec(ln3_b), _tile(t, 512))
    return x3.reshape(b, s, d)


def kernel(x, mem, positions, w_in, gdn_conv_w, gdn_a_log, gdn_dt_bias, gdn_norm_w, cmp_pos_k, cmp_pos_v, cmp_k_w1, cmp_k_w2, cmp_v_w1, cmp_v_w2, w_up_gdn, w_up_nsa, w_mix_out, ln1_g, ln1_b, xa_wq, xa_wk, xa_wv, xa_wo, ln2_g, ln2_b, ffn_w_gate, ffn_w_up, ffn_w_down, ln3_g, ln3_b):
    for l in range(w_in.shape[0]):
        x = _layer(x, mem, positions, w_in[l], gdn_conv_w[l], gdn_a_log[l], gdn_dt_bias[l], gdn_norm_w[l],
                   cmp_pos_k[l], cmp_pos_v[l], cmp_k_w1[l], cmp_k_w2[l], cmp_v_w1[l], cmp_v_w2[l], w_up_gdn[l],
                   w_up_nsa[l], w_mix_out[l], ln1_g[l], ln1_b[l], xa_wq[l], xa_wk[l], xa_wv[l], xa_wo[l], ln2_g[l],
                   ln2_b[l], ffn_w_gate[l], ffn_w_up[l], ffn_w_down[l], ln3_g[l], ln3_b[l])
    return x
```

```python
import functools

import jax
import jax.numpy as jnp
from jax import lax
from jax.experimental import pallas as pl
from jax.experimental.pallas import tpu as pltpu

F32 = jnp.float32
BF16 = jnp.bfloat16

D_MODEL = 1024
HEAD_DIM = 128
GDN_HEADS = 4
GDN_QK = GDN_HEADS * HEAD_DIM
GDN_V = GDN_HEADS * HEAD_DIM
GDN_CONV = 4
GDN_CHUNK = 64
NSA_HEADS = 4
NSA_Q = NSA_HEADS * HEAD_DIM
CMP_LEN = 32
CMP_STRIDE = 16
SLC_BLOCK = 64
SLC_TOPK = 16
WINDOW = 512
FORCE_SCORE = 1.0e4
XA_HEADS = 4
XA_HEAD_DIM = D_MODEL // XA_HEADS
ROPE_THETA = 10000.0
LN_EPS = 1e-5
RMS_EPS = 1e-6
NEG = -1.0e30
ATTN_SCALE = HEAD_DIM ** -0.5
LOG2E = 1.4426950408889634
DEPTH = 1
DN_ALPHA = (2 * DEPTH) ** 0.25

LANES = 128
CONV_HIST = 128
VT_ONES_ROWS = 16
VMEM_LIMIT_BYTES = 56 * 1024 * 1024

COL_GATE_A = 0
COL_GATE_B = 1024
COL_GQ = 2048
COL_GK = 2560
COL_GV = 3072
COL_Z = 3584
COL_NQ = 4096
COL_KV6 = 4608
N_MAIN = 5376
SM_A = 0
SM_B = 4
SM_G = 8


def _cparams(*sem):
    return pltpu.CompilerParams(dimension_semantics=sem, vmem_limit_bytes=VMEM_LIMIT_BYTES)


def _resident(shape):
    nd = len(shape)
    return pl.BlockSpec(shape, lambda *_: (0,) * nd, pipeline_mode=pl.Buffered(1))


def _layer_norm(v, g, b):
    mu = jnp.mean(v, -1, keepdims=True)
    c = v - mu
    var = jnp.mean(c * c, -1, keepdims=True)
    return c * lax.rsqrt(var + LN_EPS) * g + b


def _silu(v):
    half = 0.5 * v
    return half + half * jnp.tanh(half)


def _sigmoid(v):
    return 0.5 + 0.5 * jnp.tanh(0.5 * v)


def _dot(a, b):
    return jnp.dot(a, b, preferred_element_type=F32)


def _dot_nt(a, b):
    return lax.dot_general(a, b, (((1,), (1,)), ((), ())), preferred_element_type=F32)


def _proj_body(x_ref, w_ref, ws_ref, o_ref, os_ref, *, chunk):
    x = x_ref[...].astype(BF16)
    for c in range(0, w_ref.shape[1], chunk):
        o_ref[:, c:c + chunk] = _dot(x, w_ref[:, c:c + chunk]).astype(o_ref.dtype)
    os_ref[...] = _dot(x, ws_ref[...])


def _project(x2d, w_main, w_small, tm):
    t, d = x2d.shape
    n = w_main.shape[1]
    return pl.pallas_call(
        functools.partial(_proj_body, chunk=768),
        grid=(t // tm,),
        in_specs=[pl.BlockSpec((tm, d), lambda i: (i, 0)), _resident((d, n)), _resident((d, LANES))],
        out_specs=[pl.BlockSpec((tm, n), lambda i: (i, 0)), pl.BlockSpec((tm, LANES), lambda i: (i, 0))],
        out_shape=[jax.ShapeDtypeStruct((t, n), BF16), jax.ShapeDtypeStruct((t, LANES), F32)],
        compiler_params=_cparams("arbitrary"),
        name="proj",
    )(x2d, w_main, w_small)


def _rope_tables(pos_col, inv_row):
    n = pos_col.shape[0]
    half_n, half_d = n // 2, HEAD_DIM // 2
    low = lax.broadcasted_iota(jnp.int32, (half_n, HEAD_DIM), 1) < half_d
    ang = jnp.where(low, pos_col[:half_n], pos_col[half_n:]) * inv_row
    cos = jnp.cos(ang)
    sin = jnp.sin(ang)
    cos_sw = pltpu.roll(cos, half_d, 1)
    sin_sw = pltpu.roll(sin, half_d, 1)
    cos_full = jnp.concatenate([jnp.where(low, cos, cos_sw), jnp.where(low, cos_sw, cos)], axis=0)
    sin_signed = jnp.concatenate([jnp.where(low, -sin, sin_sw), jnp.where(low, -sin_sw, sin)], axis=0)
    return cos_full, sin_signed


def _rope(v, cos, sin_signed):
    return v * cos + pltpu.roll(v, HEAD_DIM // 2, 1) * sin_signed


def _nsa_prep_body(pos_ref, inv_ref, q_ref, ks_ref, vs_ref, kw_ref, vw_ref,
                   qt_ref, kso_ref, vst_ref, kwo_ref, vwt_ref, *, ct, tk):
    i = pl.program_id(1)
    cos, sin_s = _rope_tables(pos_ref[0], inv_ref[...])
    for h in range(NSA_HEADS):
        sl = slice(h * HEAD_DIM, (h + 1) * HEAD_DIM)
        q_rot = _rope(q_ref[0, :, sl].astype(F32), cos, sin_s) * (ATTN_SCALE * LOG2E)
        qt_ref[0, sl, :] = q_rot.T.astype(BF16)
    kso_ref[0, :, :HEAD_DIM] = _rope(ks_ref[0].astype(F32), cos, sin_s).astype(BF16)
    lane = lax.broadcasted_iota(jnp.int32, (ct, LANES), 1)
    tok = lax.broadcasted_iota(jnp.int32, (ct, LANES), 0) + i * ct
    kso_ref[0, :, HEAD_DIM:] = jnp.where(lane == tok // SLC_BLOCK, 1.0, 0.0).astype(BF16)
    kwo_ref[0] = _rope(kw_ref[0].astype(F32), cos, sin_s).astype(BF16)
    ones = jnp.ones((VT_ONES_ROWS, tk), BF16)
    for v_ref, vt_ref in ((vs_ref, vst_ref), (vw_ref, vwt_ref)):
        v_t = v_ref[0].astype(F32).T
        for n in range(ct // tk):
            vt_ref[0, n, :HEAD_DIM, :] = v_t[:, n * tk:(n + 1) * tk].astype(BF16)
            vt_ref[0, n, HEAD_DIM:, :] = ones


def _nsa_prep(proj3, posf, inv_row, ct, tk):
    b, s, _ = proj3.shape
    kv_col = COL_KV6 // HEAD_DIM
    col = lambda c: pl.BlockSpec((1, ct, HEAD_DIM), lambda bi, i: (bi, i, c))
    vrows = HEAD_DIM + VT_ONES_ROWS
    vt_spec = pl.BlockSpec((1, ct // tk, vrows, tk), lambda bi, i: (bi, i, 0, 0))
    vt_shape = jax.ShapeDtypeStruct((b, s // tk, vrows, tk), BF16)
    return pl.pallas_call(
        functools.partial(_nsa_prep_body, ct=ct, tk=tk),
        grid=(b, s // ct),
        in_specs=[
            pl.BlockSpec((1, ct, 1), lambda bi, i: (bi, i, 0)),
            _resident((1, LANES)),
            pl.BlockSpec((1, ct, NSA_Q), lambda bi, i: (bi, i, COL_NQ // NSA_Q)),
            col(kv_col + 2), col(kv_col + 3), col(kv_col + 4), col(kv_col + 5),
        ],
        out_specs=[
            pl.BlockSpec((1, NSA_Q, ct), lambda bi, i: (bi, 0, i)),
            pl.BlockSpec((1, ct, 2 * HEAD_DIM), lambda bi, i: (bi, i, 0)),
            vt_spec,
            pl.BlockSpec((1, ct, HEAD_DIM), lambda bi, i: (bi, i, 0)),
            vt_spec,
        ],
        out_shape=[
            jax.ShapeDtypeStruct((b, NSA_Q, s), BF16),
            jax.ShapeDtypeStruct((b, s, 2 * HEAD_DIM), BF16),
            vt_shape,
            jax.ShapeDtypeStruct((b, s, HEAD_DIM), BF16),
            vt_shape,
        ],
        compiler_params=_cparams("arbitrary", "arbitrary"),
        name="nsa_prep",
    )(posf, inv_row, proj3, proj3, proj3, proj3, proj3)


def _compress_body(kc_ref, vc_ref, pk_ref, pv_ref, kw1_ref, kw2_ref, vw1_ref, vw2_ref, posc_ref, inv_ref,
                   ko_ref, vo_ref):
    def comp(seg_ref, p_ref, w1_ref, w2_ref, transposed):
        seg = seg_ref[0]
        n_seg, half = seg.shape
        first = _dot(seg, w1_ref[:half, :])
        second = _dot(seg, w1_ref[half:, :])
        pos_term = _dot(jnp.broadcast_to(p_ref[...], (8, p_ref.shape[1])).astype(BF16), w1_ref[...])[0:1]
        hidden = first + pltpu.roll(second, n_seg - 1, 0) + pos_term
        act = _silu(hidden)
        if transposed:
            return _dot(w2_ref[...], act.T.astype(BF16))
        return _dot(act.astype(BF16), w2_ref[...])

    kc = comp(kc_ref, pk_ref, kw1_ref, kw2_ref, False)
    cos, sin_s = _rope_tables(posc_ref[0], inv_ref[...])
    ko_ref[0] = _rope(kc, cos, sin_s).astype(BF16)
    vo_ref[0] = comp(vc_ref, pv_ref, vw1_ref, vw2_ref, True).astype(BF16)


def _compress(kc_seg, vc_seg, pos_k, pos_v, kw1, kw2, vw1, vw2, posc, inv_row):
    b, n_seg, seg_w = kc_seg.shape
    seg_spec = pl.BlockSpec((1, n_seg, seg_w), lambda bi: (bi, 0, 0))
    return pl.pallas_call(
        _compress_body,
        grid=(b,),
        in_specs=[seg_spec, seg_spec, _resident(pos_k.shape), _resident(pos_v.shape), _resident(kw1.shape),
                  _resident(kw2.shape), _resident(vw1.shape), _resident(vw2.shape),
                  pl.BlockSpec((1, n_seg, 1), lambda bi: (bi, 0, 0)), _resident((1, LANES))],
        out_specs=[pl.BlockSpec((1, n_seg, HEAD_DIM), lambda bi: (bi, 0, 0)),
                   pl.BlockSpec((1, HEAD_DIM, n_seg), lambda bi: (bi, 0, 0))],
        out_shape=[jax.ShapeDtypeStruct((b, n_seg, HEAD_DIM), BF16),
                   jax.ShapeDtypeStruct((b, HEAD_DIM, n_seg), BF16)],
        compiler_params=_cparams("arbitrary"),
        name="compress",
    )(kc_seg, vc_seg, pos_k, pos_v, kw1, kw2, vw1, vw2, posc, inv_row)


def _split3(v):
    hi = v.astype(BF16)
    r1 = v - hi.astype(F32)
    mid = r1.astype(BF16)
    lo = (r1 - mid.astype(F32)).astype(BF16)
    return hi, mid, lo


def _nsa_attn_body(qt_ref, sm_ref, kc_ref, vct_ref, ks_ref, vst_ref, kw_ref, vwt_ref, o_ref,
                   m_sc, acc_sc, m2_sc, acc2_sc, s_buf, *, tq, n_cmp_pad, n_slc, top_k):
    qi = pl.program_id(1)
    t0 = qi * tq
    hq = NSA_HEADS * tq
    q_t = jnp.concatenate([qt_ref[0, h * HEAD_DIM:(h + 1) * HEAD_DIM, :] for h in range(NSA_HEADS)], axis=1)

    def lane_query(shape):
        return lax.broadcasted_iota(jnp.int32, shape, 1) & (tq - 1)

    s_c = _dot(kc_ref[0], q_t)
    n_id = lax.broadcasted_iota(jnp.int32, (n_cmp_pad, hq), 0)
    valid_c = (n_id * CMP_STRIDE + (CMP_LEN - 1)) <= t0 + lane_query((n_cmp_pad, hq))
    m_c = jnp.max(jnp.where(valid_c, s_c, NEG), 0, keepdims=True)
    e_c = jnp.where(valid_c, jnp.exp2(s_c - m_c), 0.0)
    l_c = jnp.sum(e_c, 0, keepdims=True)
    p_c = e_c * (1.0 / jnp.where(l_c > 0.0, l_c, 1.0))
    o_c = _dot(vct_ref[0], p_c.astype(BF16))

    p_sum = p_c[:, 0:tq]
    for h in range(1, NSA_HEADS):
        p_sum = p_sum + p_c[:, h * tq:(h + 1) * tq]
    oj = lax.broadcasted_iota(jnp.int32, (LANES, n_cmp_pad), 0) * SLC_BLOCK
    on = lax.broadcasted_iota(jnp.int32, (LANES, n_cmp_pad), 1) * CMP_STRIDE
    overlap_t = jnp.where((on < oj + SLC_BLOCK) & (on + CMP_LEN > oj), 1.0, 0.0).astype(BF16)
    imp = sum(_dot(overlap_t, part) for part in _split3(p_sum))

    j = lax.broadcasted_iota(jnp.int32, (LANES, tq), 0)
    cur = (t0 + lax.broadcasted_iota(jnp.int32, (LANES, tq), 1)) // SLC_BLOCK
    forced = (j == 0) | (j == cur) | (j == cur - 1)
    cand = jnp.where(jnp.logical_not(forced) & (j <= cur) & (j < n_slc), imp, -1.0)
    sel = jnp.where(forced, 1.0, 0.0)
    for _ in range(max(top_k - 3, 0)):
        best = jnp.max(cand, 0, keepdims=True)
        idx = jnp.min(jnp.where(cand == best, j, LANES), 0, keepdims=True)
        hit = j == idx
        sel = jnp.where(hit & (best >= 0.0), 1.0, sel)
        cand = jnp.where(hit, -2.0, cand)
    bias = jnp.where(sel > 0.0, 0.0, NEG).astype(BF16)
    q_aug = jnp.concatenate([q_t, jnp.concatenate([bias] * NSA_HEADS, axis=1)], axis=0)

    def tile_scores(k_ref, n, q_op, mask):
        k_tile = k_ref[0, pl.ds(pl.multiple_of(n * tq, tq), tq), :]
        s = _dot(k_tile, q_op)
        return (s if mask is None else jnp.where(mask, s, NEG)).astype(BF16)

    def flash_update(m_ref, acc_ref, tiles):
        m_old = m_ref[...]
        m_loc = functools.reduce(jnp.maximum, [jnp.max(s, 0, keepdims=True) for s, _ in tiles])
        m_new = jnp.maximum(m_old, m_loc.astype(F32))
        m_b = m_new.astype(BF16)
        p = jnp.concatenate([jnp.exp2(s - m_b) for s, _ in tiles], axis=0)
        vt = jnp.concatenate([vt_tile for _, vt_tile in tiles], axis=1)
        acc_ref[...] = jnp.exp2(m_old - m_new) * acc_ref[...] + _dot(vt, p)
        m_ref[...] = m_new

    def flash_init(m_ref, acc_ref):
        m_ref[...] = jnp.full(m_ref.shape, NEG, F32)
        acc_ref[...] = jnp.zeros(acc_ref.shape, F32)

    kl = lax.broadcasted_iota(jnp.int32, (tq, hq), 0)
    ql = lane_query((tq, hq))
    causal = kl <= ql

    flash_init(m_sc, acc_sc)
    n_pairs = qi // 2

    def stage_scores(slot, pair):
        keys = ks_ref[0, pl.ds(pl.multiple_of(pair * (2 * tq), 2 * tq), 2 * tq), :]
        s_buf[slot] = _dot(keys, q_aug).astype(BF16)

    def consume(slot, pair):
        vt = jnp.concatenate([vst_ref[0, 2 * pair], vst_ref[0, 2 * pair + 1]], axis=1)
        flash_update(m_sc, acc_sc, [(s_buf[slot], vt)])

    @pl.when(n_pairs > 0)
    def _():
        stage_scores(0, 0)

    def two_pairs(j, carry):
        stage_scores(1, 2 * j + 1)
        consume(0, 2 * j)
        stage_scores(0, jnp.minimum(2 * j + 2, n_pairs - 1))
        consume(1, 2 * j + 1)
        return carry

    lax.fori_loop(0, n_pairs // 2, two_pairs, 0)

    @pl.when(n_pairs % 2 == 1)
    def _():
        consume(0, n_pairs - 1)

    odd = (qi % 2) == 1
    prev = jnp.maximum(qi - 1, 0)
    flash_update(m_sc, acc_sc, [
        (tile_scores(ks_ref, prev, q_aug, jnp.broadcast_to(odd, (tq, hq))), vst_ref[0, prev]),
        (tile_scores(ks_ref, qi, q_aug, causal), vst_ref[0, qi])])

    flash_init(m2_sc, acc2_sc)
    window_tiles = []
    for back in range(0, WINDOW // tq + 1):
        n = jnp.maximum(qi - back, 0)
        if back == 0:
            mask = causal
        elif back * tq == WINDOW:
            mask = (kl > ql) & (qi >= back)
        else:
            mask = jnp.broadcast_to(qi >= back, (tq, hq))
        window_tiles.append((tile_scores(kw_ref, n, q_t, mask), vwt_ref[0, n]))
    flash_update(m2_sc, acc2_sc, window_tiles)

    acc_s, acc_w = acc_sc[...], acc2_sc[...]
    o_s = acc_s[:HEAD_DIM] * (1.0 / acc_s[HEAD_DIM:HEAD_DIM + 1])
    o_w = acc_w[:HEAD_DIM] * (1.0 / acc_w[HEAD_DIM:HEAD_DIM + 1])
    gates_t = _sigmoid(sm_ref[0]).T
    eye = jnp.where(lax.broadcasted_iota(jnp.int32, (tq, tq), 0) == lax.broadcasted_iota(jnp.int32, (tq, tq), 1),
                    1.0, 0.0).astype(BF16)
    for h in range(NSA_HEADS):
        cols = slice(h * tq, (h + 1) * tq)
        c0 = SM_G + 3 * h
        o_t = (gates_t[c0:c0 + 1, :] * o_c[:, cols] + gates_t[c0 + 1:c0 + 2, :] * o_s[:, cols]
               + gates_t[c0 + 2:c0 + 3, :] * o_w[:, cols])
        o_ref[0, :, h * HEAD_DIM:(h + 1) * HEAD_DIM] = _dot_nt(eye, o_t.astype(BF16)).astype(o_ref.dtype)


def _nsa_attention(q_t, small3, k_cmp, v_cmp_t, ks_aug, vs_t, kw_rope, vw_t, tq):
    b, _, s = q_t.shape
    n_cmp_pad = k_cmp.shape[1]
    n_slc = s // SLC_BLOCK
    top_k = min(SLC_TOPK, n_slc)
    vrows = vs_t.shape[2]
    assert WINDOW % tq == 0 and vs_t.shape[-1] == tq
    batch_spec = lambda shape: pl.BlockSpec((1,) + shape, lambda bi, i: (bi,) + (0,) * len(shape))
    hq = NSA_HEADS * tq
    flash_scratch = [pltpu.VMEM((1, hq), F32), pltpu.VMEM((vrows, hq), F32)]
    return pl.pallas_call(
        functools.partial(_nsa_attn_body, tq=tq, n_cmp_pad=n_cmp_pad, n_slc=n_slc, top_k=top_k),
        grid=(b, s // tq),
        in_specs=[
            pl.BlockSpec((1, NSA_Q, tq), lambda bi, i: (bi, 0, i)),
            pl.BlockSpec((1, tq, LANES), lambda bi, i: (bi, i, 0)),
            batch_spec((n_cmp_pad, HEAD_DIM)),
            batch_spec((HEAD_DIM, n_cmp_pad)),
            batch_spec((s, 2 * HEAD_DIM)),
            batch_spec((s // tq, vrows, tq)),
            batch_spec((s, HEAD_DIM)),
            batch_spec((s // tq, vrows, tq)),
        ],
        out_specs=pl.BlockSpec((1, tq, NSA_Q), lambda bi, i: (bi, i, 0)),
        out_shape=jax.ShapeDtypeStruct((b, s, NSA_Q), BF16),
        scratch_shapes=flash_scratch + flash_scratch + [pltpu.VMEM((2, 2 * tq, hq), BF16)],
        compiler_params=_cparams("arbitrary", "arbitrary"),
        name="nsa_attn",
    )(q_t, small3, k_cmp, v_cmp_t, ks_aug, vs_t, kw_rope, vw_t)


def _gdn_body(xq_ref, xk_ref, xv_ref, z_ref, sm_ref, cw_ref, hp_ref, nw_ref, o_ref,
              xpad_sc, conv_sc, state_sc, qp_sc, op_sc, mp_sc, nn_sc, eg_sc, *, ct):
    i = pl.program_id(1)
    c = GDN_CHUNK
    nc = ct // c

    @pl.when(i == 0)
    def _():
        xpad_sc[:, 0:CONV_HIST, :] = jnp.zeros((3, CONV_HIST, GDN_QK), BF16)
        state_sc[...] = jnp.zeros_like(state_sc)

    row = lax.broadcasted_iota(jnp.int32, (ct, LANES), 0)
    row_in_chunk = row & (c - 1)
    ii = lax.broadcasted_iota(jnp.int32, (nc, c, c), 1)
    jj = lax.broadcasted_iota(jnp.int32, (nc, c, c), 2)

    blk = CONV_HIST
    for part, x_ref in enumerate((xq_ref, xk_ref, xv_ref)):
        wcols = slice(part * GDN_QK, (part + 1) * GDN_QK)
        xpad_sc[part, blk:, :] = x_ref[0]
        for r in range(ct // blk):
            window = xpad_sc[part, r * blk:(r + 2) * blk, :].astype(F32)
            acc = window[blk:] * cw_ref[GDN_CONV - 1:GDN_CONV, wcols]
            for k in range(1, GDN_CONV):
                acc = acc + window[blk - k:2 * blk - k] * cw_ref[GDN_CONV - 1 - k:GDN_CONV - k, wcols]
            conv_sc[part, r * blk:(r + 1) * blk, :] = _silu(acc)
        xpad_sc[part, 0:blk, :] = xpad_sc[part, ct:ct + blk, :]

    def conv_silu(part, h):
        return conv_sc[part, :, h * HEAD_DIM:(h + 1) * HEAD_DIM]

    def l2n(v):
        return v * lax.rsqrt(jnp.sum(v * v, -1, keepdims=True) + RMS_EPS)

    sm = sm_ref[0]
    sp_in = sm + hp_ref[1:2, :]
    softplus = jnp.maximum(sp_in, 0.0) + jnp.log(1.0 + jnp.exp(-jnp.abs(sp_in)))
    g_all = -jnp.exp(hp_ref[0:1, :]) * softplus
    beta_all = _sigmoid(sm)
    for h in range(GDN_HEADS):
        q = l2n(conv_silu(0, h)) * (HEAD_DIM ** -0.5)
        k = l2n(conv_silu(1, h))
        v = conv_silu(2, h)
        g = g_all[:, SM_A + h:SM_A + h + 1]
        beta = beta_all[:, SM_B + h:SM_B + h + 1]
        gc = jnp.broadcast_to(g, (ct, LANES))
        step = 1
        while step < c:
            gc = gc + jnp.where(row_in_chunk >= step, pltpu.roll(gc, step, 0), 0.0)
            step *= 2
        egc = jnp.exp(gc)
        kb = k * beta
        vb = v * beta
        gc3 = gc.reshape(nc, c, LANES)
        g_col = gc3[:, :, :c]
        g_row = jnp.swapaxes(gc3, 1, 2)[:, :c, :]
        decay = jnp.where(ii >= jj, jnp.exp(jnp.minimum(g_col - g_row, 0.0)), 0.0)
        k3 = k.reshape(nc, c, LANES).astype(BF16)
        kk = jnp.einsum("nid,njd->nij", kb.reshape(nc, c, LANES).astype(BF16), k3, preferred_element_type=F32)
        a_mat = jnp.where(ii > jj, kk * decay, 0.0)
        n_mat = -a_mat
        pw = a_mat
        sq = 2
        while sq < c:
            pwb = pw.astype(BF16)
            pw = jnp.einsum("nij,njk->nik", pwb, pwb, preferred_element_type=F32)
            n_mat = n_mat + pw + jnp.einsum("nij,njk->nik", n_mat.astype(BF16), pw.astype(BF16),
                                            preferred_element_type=F32)
            sq *= 2
        nb = n_mat.astype(BF16)
        vb3 = vb.reshape(nc, c, LANES)
        kbg3 = (kb * egc).reshape(nc, c, LANES)
        u = vb3 + jnp.einsum("nij,njd->nid", nb, vb3.astype(BF16), preferred_element_type=F32)
        w = kbg3 + jnp.einsum("nij,njd->nid", nb, kbg3.astype(BF16), preferred_element_type=F32)
        attn = jnp.einsum("nid,njd->nij", q.reshape(nc, c, LANES).astype(BF16), k3,
                          preferred_element_type=F32) * decay
        g_last = gc3[:, c - 1:c, :]
        wb = w.astype(BF16)
        ub = u.astype(BF16)
        ab = attn.astype(BF16)
        kt_t = jnp.swapaxes(k.reshape(nc, c, LANES) * jnp.exp(g_last - gc3), 1, 2).astype(BF16)
        qd3 = (q * egc).reshape(nc, c, LANES)
        qp = qd3 - jnp.einsum("nij,njd->nid", ab, wb, preferred_element_type=F32)
        op = jnp.einsum("nij,nje->nie", ab, ub, preferred_element_type=F32)
        qp_sc[h] = qp.reshape(ct, LANES).astype(BF16)
        op_sc[h] = op.reshape(ct, LANES)
        mp_sc[h] = jnp.einsum("ndi,nie->nde", kt_t, wb, preferred_element_type=F32).astype(BF16)
        nn_sc[h] = jnp.einsum("ndi,nie->nde", kt_t, ub, preferred_element_type=F32)
        eg_sc[h] = jnp.broadcast_to(jnp.exp(g_last), (nc, 8, LANES))

    nw = nw_ref[...]

    def chunk_step(n, carry):
        rows = pl.ds(pl.multiple_of(n * c, c), c)
        for h in range(GDN_HEADS):
            state = state_sc[h]
            sb = state.astype(BF16)
            o = _dot(qp_sc[h, rows, :], sb) + op_sc[h, rows, :]
            chunk_decay = eg_sc[h, n][0:1]
            state_sc[h] = state * chunk_decay - _dot(mp_sc[h, n], sb) + nn_sc[h, n]
            zf = z_ref[0, rows, h * HEAD_DIM:(h + 1) * HEAD_DIM].astype(F32)
            o = o * lax.rsqrt(jnp.mean(o * o, -1, keepdims=True) + RMS_EPS) * nw * _silu(zf)
            o_ref[0, rows, h * HEAD_DIM:(h + 1) * HEAD_DIM] = o.astype(o_ref.dtype)
        return carry

    lax.fori_loop(0, nc, chunk_step, 0)


def _gdn(proj3, small3, conv_w, hp, norm_w, ct):
    b, s, _ = proj3.shape
    cb = lambda col: col // GDN_QK
    blk = lambda col: pl.BlockSpec((1, ct, GDN_QK), lambda bi, i: (bi, i, cb(col)))
    hsh = (GDN_HEADS, ct, LANES)
    nc = ct // GDN_CHUNK
    return pl.pallas_call(
        functools.partial(_gdn_body, ct=ct),
        grid=(b, s // ct),
        in_specs=[blk(COL_GQ), blk(COL_GK), blk(COL_GV), blk(COL_Z),
                  pl.BlockSpec((1, ct, LANES), lambda bi, i: (bi, i, 0)),
                  _resident(conv_w.shape), _resident(hp.shape), _resident(norm_w.shape)],
        out_specs=pl.BlockSpec((1, ct, GDN_V), lambda bi, i: (bi, i, 0)),
        out_shape=jax.ShapeDtypeStruct((b, s, GDN_V), BF16),
        scratch_shapes=[
            pltpu.VMEM((3, ct + CONV_HIST, GDN_QK), BF16),
            pltpu.VMEM((3, ct, GDN_QK), F32),
            pltpu.VMEM((GDN_HEADS, HEAD_DIM, HEAD_DIM), F32),
            pltpu.VMEM(hsh, BF16),
            pltpu.VMEM(hsh, F32),
            pltpu.VMEM((GDN_HEADS, nc, HEAD_DIM, HEAD_DIM), BF16),
            pltpu.VMEM((GDN_HEADS, nc, HEAD_DIM, HEAD_DIM), F32),
            pltpu.VMEM((GDN_HEADS, nc, 8, LANES), F32),
        ],
        compiler_params=_cparams("arbitrary", "arbitrary"),
        name="gdn",
    )(proj3, proj3, proj3, proj3, small3, conv_w, hp, norm_w)


def _merge_body(x_ref, oa_ref, ob_ref, ga_ref, gb_ref, wa_ref, wb_ref, wm_ref, g_ref, b_ref, o_ref):
    ya = _dot(oa_ref[...], wa_ref[...])
    yb = _dot(ob_ref[...], wb_ref[...])
    y = _sigmoid(ga_ref[...].astype(F32)) * ya + _sigmoid(gb_ref[...].astype(F32)) * yb
    out = _dot(y.astype(BF16), wm_ref[...])
    o_ref[...] = _layer_norm(DN_ALPHA * x_ref[...] + out, g_ref[...], b_ref[...])


def _merge(x2d, oa, ob, proj2d, wa, wb, wm, g, bias, tm):
    t, d = x2d.shape
    row = lambda w, cbi=0: pl.BlockSpec((tm, w), lambda i: (i, cbi))
    return pl.pallas_call(
        _merge_body,
        grid=(t // tm,),
        in_specs=[row(d), row(GDN_V), row(NSA_Q), row(d, COL_GATE_A // d), row(d, COL_GATE_B // d),
                  _resident(wa.shape), _resident(wb.shape), _resident(wm.shape), _resident(g.shape),
                  _resident(bias.shape)],
        out_specs=row(d),
        out_shape=jax.ShapeDtypeStruct((t, d), F32),
        compiler_params=_cparams("arbitrary"),
        name="merge",
    )(x2d, oa, ob, proj2d, proj2d, wa, wb, wm, g, bias)


def _memkv_body(m_ref, wk_ref, wv_ref, k_ref, v_ref):
    m = m_ref[...].astype(BF16)
    k_ref[...] = _dot(m, wk_ref[...]).astype(BF16)
    v_ref[...] = _dot(m, wv_ref[...]).astype(BF16)


def _memkv(mem2d, wk, wv, tm):
    t, d = mem2d.shape
    row = pl.BlockSpec((tm, d), lambda i: (i, 0))
    return pl.pallas_call(
        _memkv_body,
        grid=(t // tm,),
        in_specs=[row, _resident(wk.shape), _resident(wv.shape)],
        out_specs=[row, row],
        out_shape=[jax.ShapeDtypeStruct((t, d), BF16)] * 2,
        compiler_params=_cparams("arbitrary"),
        name="memkv",
    )(mem2d, wk, wv)


def _xattn_body(x_ref, k_ref, v_ref, wq_ref, wo_ref, g_ref, b_ref, o_ref):
    x = x_ref[0]
    q = _dot(x.astype(BF16), wq_ref[...]).astype(BF16)
    heads = []
    for h in range(XA_HEADS):
        sl = slice(h * XA_HEAD_DIM, (h + 1) * XA_HEAD_DIM)
        s = _dot_nt(q[:, sl], k_ref[0, :, sl]) * (XA_HEAD_DIM ** -0.5)
        m = jnp.max(s, -1, keepdims=True)
        e = jnp.exp(s - m)
        p = e * (1.0 / jnp.sum(e, -1, keepdims=True))
        heads.append(_dot(p.astype(BF16), v_ref[0, :, sl]).astype(BF16))
    o = jnp.concatenate(heads, axis=1)
    out = _dot(o, wo_ref[...])
    o_ref[0] = _layer_norm(DN_ALPHA * x + out, g_ref[...], b_ref[...])


def _xattn(x3, k3, v3, wq, wo, g, bias, tm):
    b, s, d = x3.shape
    m = k3.shape[1]
    return pl.pallas_call(
        _xattn_body,
        grid=(b, s // tm),
        in_specs=[pl.BlockSpec((1, tm, d), lambda bi, i: (bi, i, 0)),
                  pl.BlockSpec((1, m, d), lambda bi, i: (bi, 0, 0)),
                  pl.BlockSpec((1, m, d), lambda bi, i: (bi, 0, 0)),
                  _resident(wq.shape), _resident(wo.shape), _resident(g.shape), _resident(bias.shape)],
        out_specs=pl.BlockSpec((1, tm, d), lambda bi, i: (bi, i, 0)),
        out_shape=jax.ShapeDtypeStruct((b, s, d), F32),
        compiler_params=_cparams("arbitrary", "arbitrary"),
        name="xattn",
    )(x3, k3, v3, wq, wo, g, bias)


def _ffn_body(x_ref, wg_ref, wu_ref, wd_ref, g_ref, b_ref, o_ref, *, chunk):
    x = x_ref[...]
    xb = x.astype(BF16)
    out = None
    for c in range(0, wg_ref.shape[1], chunk):
        hid = _silu(_dot(xb, wg_ref[:, c:c + chunk])) * _dot(xb, wu_ref[:, c:c + chunk])
        part = _dot(hid.astype(BF16), wd_ref[c:c + chunk, :])
        out = part if out is None else out + part
    o_ref[...] = _layer_norm(DN_ALPHA * x + out, g_ref[...], b_ref[...])


def _ffn(x2d, wg, wu, wd, g, bias, tm):
    t, d = x2d.shape
    dff = wg.shape[1]
    chunk = dff // 2 if (dff // 2) % LANES == 0 else dff
    row = pl.BlockSpec((tm, d), lambda i: (i, 0))
    return pl.pallas_call(
        functools.partial(_ffn_body, chunk=chunk),
        grid=(t // tm,),
        in_specs=[row, _resident(wg.shape), _resident(wu.shape), _resident(wd.shape), _resident(g.shape),
                  _resident(bias.shape)],
        out_specs=row,
        out_shape=jax.ShapeDtypeStruct((t, d), F32),
        compiler_params=_cparams("arbitrary"),
        name="ffn",
    )(x2d, wg, wu, wd, g, bias)


def _tile(n, pref):
    t = min(n, pref)
    assert n % t == 0, (n, pref)
    return t


def _layer(x, mem, positions, w_in, conv_w, a_log, dt_bias, gdn_norm_w, cmp_pos_k, cmp_pos_v, cmp_k_w1, cmp_k_w2,
           cmp_v_w1, cmp_v_w2, w_up_gdn, w_up_nsa, w_mix_out, ln1_g, ln1_b, xa_wq, xa_wk, xa_wv, xa_wo, ln2_g,
           ln2_b, ffn_w_gate, ffn_w_up, ffn_w_down, ln3_g, ln3_b):
    b, s, d = x.shape
    t = b * s
    assert d == D_MODEL and s % WINDOW == 0 and s // SLC_BLOCK <= LANES and s >= 2 * WINDOW

    o_qkv, o_z = 0, 3 * GDN_QK
    o_a = o_z + GDN_V
    o_b = o_a + GDN_HEADS
    o_qn = o_b + GDN_HEADS
    o_kv6 = o_qn + NSA_Q
    o_gn = o_kv6 + 6 * HEAD_DIM
    o_ga = o_gn + 3 * NSA_HEADS
    o_gb = o_ga + D_MODEL
    w_main = jnp.concatenate([w_in[:, o_ga:o_gb], w_in[:, o_gb:o_gb + D_MODEL], w_in[:, o_qkv:o_z], w_in[:, o_z:o_a],
                              w_in[:, o_qn:o_kv6], w_in[:, o_kv6:o_gn]], axis=1).astype(BF16)
    n_small = 2 * GDN_HEADS + 3 * NSA_HEADS
    w_small = jnp.concatenate([w_in[:, o_a:o_qn], w_in[:, o_gn:o_ga], jnp.zeros((d, LANES - n_small), w_in.dtype)],
                              axis=1).astype(BF16)

    x2d = x.reshape(t, d)
    proj2d, small2d = _project(x2d, w_main, w_small, _tile(t, 512))
    proj3 = proj2d.reshape(b, s, N_MAIN)
    small3 = small2d.reshape(b, s, LANES)

    hp = jnp.zeros((8, LANES), F32).at[0, :GDN_HEADS].set(a_log.astype(F32)).at[1, :GDN_HEADS].set(
        dt_bias.astype(F32))
    o_gdn = _gdn(proj3, small3, conv_w.astype(F32), hp, gdn_norm_w.reshape(1, HEAD_DIM).astype(F32), _tile(s, 512))

    half = HEAD_DIM // 2
    inv = ROPE_THETA ** (-jnp.arange(half, dtype=F32) / half)
    inv_row = jnp.concatenate([inv, inv]).reshape(1, HEAD_DIM)
    posf = positions.astype(F32)
    tq = _tile(s, 256)
    q_t, ks_aug, vs_t, kw_rope, vw_t = _nsa_prep(proj3, posf.reshape(b, s, 1), inv_row, _tile(s, 512), tq)

    n_seg = s // CMP_STRIDE
    seg_w = CMP_STRIDE * HEAD_DIM
    kc_seg = proj3[:, :, COL_KV6:COL_KV6 + HEAD_DIM].reshape(b, n_seg, seg_w)
    vc_seg = proj3[:, :, COL_KV6 + HEAD_DIM:COL_KV6 + 2 * HEAD_DIM].reshape(b, n_seg, seg_w)
    posc = posf[:, CMP_LEN - 1::CMP_STRIDE]
    posc = jnp.concatenate([posc, posc[:, -1:]], axis=1).reshape(b, n_seg, 1)
    k_cmp, v_cmp_t = _compress(kc_seg, vc_seg, cmp_pos_k.reshape(1, CMP_LEN * HEAD_DIM).astype(F32),
                             cmp_pos_v.reshape(1, CMP_LEN * HEAD_DIM).astype(F32), cmp_k_w1.astype(BF16),
                             cmp_k_w2.astype(BF16), cmp_v_w1.astype(BF16), cmp_v_w2.T.astype(BF16), posc, inv_row)
    o_nsa = _nsa_attention(q_t, small3, k_cmp, v_cmp_t, ks_aug, vs_t, kw_rope, vw_t, tq)

    vec = lambda p: p.reshape(1, d).astype(F32)
    x1 = _merge(x2d, o_gdn.reshape(t, GDN_V), o_nsa.reshape(t, NSA_Q), proj2d, w_up_gdn.astype(BF16),
                w_up_nsa.astype(BF16), w_mix_out.astype(BF16), vec(ln1_g), vec(ln1_b), _tile(t, 1024))
    mlen = mem.shape[1]
    mk, mv = _memkv(mem.reshape(b * mlen, d), xa_wk.astype(BF16), xa_wv.astype(BF16), _tile(b * mlen, 512))
    x2 = _xattn(x1.reshape(b, s, d), mk.reshape(b, mlen, d), mv.reshape(b, mlen, d), xa_wq.astype(BF16),
                xa_wo.astype(BF16), vec(ln2_g), vec(ln2_b), _tile(s, 1024))
    x3 = _ffn(x2.reshape(t, d), ffn_w_gate.astype(BF16), ffn_w_up.astype(BF16), ffn_w_down.astype(BF16),
              vec(ln3_g), vec(ln3_b), _tile(t, 512))
    return x3.reshape(b, s, d)


def kernel(x, mem, positions, w_in, gdn_conv_w, gdn_a_log, gdn_dt_bias, gdn_norm_w, cmp_pos_k, cmp_pos_v, cmp_k_w1, cmp_k_w2, cmp_v_w1, cmp_v_w2, w_up_gdn, w_up_nsa, w_mix_out, ln1_g, ln1_b, xa_wq, xa_wk, xa_wv, xa_wo, ln2_g, ln2_b, ffn_w_gate, ffn_w_up, ffn_w_down, ln3_g, ln3_b):
    for l in range(w_in.shape[0]):
        x = _layer(x, mem, positions, w_in[l], gdn_conv_w[l], gdn_a_log[l], gdn_dt_bias[l], gdn_norm_w[l],
                   cmp_pos_k[l], cmp_pos_v[l], cmp_k_w1[l], cmp_k_w2[l], cmp_v_w1[l], cmp_v_w2[l], w_up_gdn[l],
                   w_up_nsa[l], w_mix_out[l], ln1_g[l], ln1_b[l], xa_wq[l], xa_wk[l], xa_wv[l], xa_wo[l], ln2_g[l],
                   ln2_b[l], ffn_w_gate[l], ffn_w_up[l], ffn_w_down[l], ln3_g[l], ln3_b[l])
    return x
```

```python
import functools

import jax
import jax.numpy as jnp
from jax import lax
from jax.experimental import pallas as pl
from jax.experimental.pallas import tpu as pltpu

F32 = jnp.float32
BF16 = jnp.bfloat16

D_MODEL = 1024
HEAD_DIM = 128
GDN_HEADS = 4
GDN_QK = GDN_HEADS * HEAD_DIM
GDN_V = GDN_HEADS * HEAD_DIM
GDN_CONV = 4
GDN_CHUNK = 64
NSA_HEADS = 4
NSA_Q = NSA_HEADS * HEAD_DIM
CMP_LEN = 32
CMP_STRIDE = 16
SLC_BLOCK = 64
SLC_TOPK = 16
WINDOW = 512
FORCE_SCORE = 1.0e4
XA_HEADS = 4
XA_HEAD_DIM = D_MODEL // XA_HEADS
ROPE_THETA = 10000.0
LN_EPS = 1e-5
RMS_EPS = 1e-6
NEG = -1.0e30
ATTN_SCALE = HEAD_DIM ** -0.5
LOG2E = 1.4426950408889634
DEPTH = 1
DN_ALPHA = (2 * DEPTH) ** 0.25

LANES = 128
CONV_HIST = 128
VT_ONES_ROWS = 16
VMEM_LIMIT_BYTES = 56 * 1024 * 1024

COL_GATE_A = 0
COL_GATE_B = 1024
COL_GQ = 2048
COL_GK = 2560
COL_GV = 3072
COL_Z = 3584
COL_NQ = 4096
COL_KV6 = 4608
N_MAIN = 5376
SM_A = 0
SM_B = 4
SM_G = 8


def _cparams(*sem):
    return pltpu.CompilerParams(dimension_semantics=sem, vmem_limit_bytes=VMEM_LIMIT_BYTES)


def _resident(shape):
    nd = len(shape)
    return pl.BlockSpec(shape, lambda *_: (0,) * nd, pipeline_mode=pl.Buffered(1))


def _layer_norm(v, g, b):
    mu = jnp.mean(v, -1, keepdims=True)
    c = v - mu
    var = jnp.mean(c * c, -1, keepdims=True)
    return c * lax.rsqrt(var + LN_EPS) * g + b


def _silu(v):
    half = 0.5 * v
    return half + half * jnp.tanh(half)


def _sigmoid(v):
    return 0.5 + 0.5 * jnp.tanh(0.5 * v)


def _dot(a, b):
    return jnp.dot(a, b, preferred_element_type=F32)


def _dot_nt(a, b):
    return lax.dot_general(a, b, (((1,), (1,)), ((), ())), preferred_element_type=F32)


def _proj_body(x_ref, w_ref, ws_ref, o_ref, os_ref, *, chunk):
    x = x_ref[...].astype(BF16)
    for c in range(0, w_ref.shape[1], chunk):
        o_ref[:, c:c + chunk] = _dot(x, w_ref[:, c:c + chunk]).astype(o_ref.dtype)
    os_ref[...] = _dot(x, ws_ref[...])


def _project(x2d, w_main, w_small, tm):
    t, d = x2d.shape
    n = w_main.shape[1]
    return pl.pallas_call(
        functools.partial(_proj_body, chunk=768),
        grid=(t // tm,),
        in_specs=[pl.BlockSpec((tm, d), lambda i: (i, 0)), _resident((d, n)), _resident((d, LANES))],
        out_specs=[pl.BlockSpec((tm, n), lambda i: (i, 0)), pl.BlockSpec((tm, LANES), lambda i: (i, 0))],
        out_shape=[jax.ShapeDtypeStruct((t, n), BF16), jax.ShapeDtypeStruct((t, LANES), F32)],
        compiler_params=_cparams("arbitrary"),
        name="proj",
    )(x2d, w_main, w_small)


def _rope_tables(pos_col, inv_row):
    n = pos_col.shape[0]
    half_n, half_d = n // 2, HEAD_DIM // 2
    low = lax.broadcasted_iota(jnp.int32, (half_n, HEAD_DIM), 1) < half_d
    ang = jnp.where(low, pos_col[:half_n], pos_col[half_n:]) * inv_row
    cos = jnp.cos(ang)
    sin = jnp.sin(ang)
    cos_sw = pltpu.roll(cos, half_d, 1)
    sin_sw = pltpu.roll(sin, half_d, 1)
    cos_full = jnp.concatenate([jnp.where(low, cos, cos_sw), jnp.where(low, cos_sw, cos)], axis=0)
    sin_signed = jnp.concatenate([jnp.where(low, -sin, sin_sw), jnp.where(low, -sin_sw, sin)], axis=0)
    return cos_full, sin_signed


def _rope(v, cos, sin_signed):
    return v * cos + pltpu.roll(v, HEAD_DIM // 2, 1) * sin_signed


def _nsa_prep_body(pos_ref, inv_ref, q_ref, ks_ref, vs_ref, kw_ref, vw_ref,
                   qt_ref, kso_ref, vst_ref, kwo_ref, vwt_ref, *, ct, tk):
    i = pl.program_id(1)
    cos, sin_s = _rope_tables(pos_ref[0], inv_ref[...])
    for h in range(NSA_HEADS):
        sl = slice(h * HEAD_DIM, (h + 1) * HEAD_DIM)
        q_rot = _rope(q_ref[0, :, sl].astype(F32), cos, sin_s) * (ATTN_SCALE * LOG2E)
        qt_ref[0, sl, :] = q_rot.T.astype(BF16)
    kso_ref[0, :, :HEAD_DIM] = _rope(ks_ref[0].astype(F32), cos, sin_s).astype(BF16)
    lane = lax.broadcasted_iota(jnp.int32, (ct, LANES), 1)
    tok = lax.broadcasted_iota(jnp.int32, (ct, LANES), 0) + i * ct
    kso_ref[0, :, HEAD_DIM:] = jnp.where(lane == tok // SLC_BLOCK, 1.0, 0.0).astype(BF16)
    kwo_ref[0] = _rope(kw_ref[0].astype(F32), cos, sin_s).astype(BF16)
    ones = jnp.ones((VT_ONES_ROWS, tk), BF16)
    for v_ref, vt_ref in ((vs_ref, vst_ref), (vw_ref, vwt_ref)):
        v_t = v_ref[0].astype(F32).T
        for n in range(ct // tk):
            vt_ref[0, n, :HEAD_DIM, :] = v_t[:, n * tk:(n + 1) * tk].astype(BF16)
            vt_ref[0, n, HEAD_DIM:, :] = ones


def _nsa_prep(proj3, posf, inv_row, ct, tk):
    b, s, _ = proj3.shape
    kv_col = COL_KV6 // HEAD_DIM
    col = lambda c: pl.BlockSpec((1, ct, HEAD_DIM), lambda bi, i: (bi, i, c))
    vrows = HEAD_DIM + VT_ONES_ROWS
    vt_spec = pl.BlockSpec((1, ct // tk, vrows, tk), lambda bi, i: (bi, i, 0, 0))
    vt_shape = jax.ShapeDtypeStruct((b, s // tk, vrows, tk), BF16)
    return pl.pallas_call(
        functools.partial(_nsa_prep_body, ct=ct, tk=tk),
        grid=(b, s // ct),
        in_specs=[
            pl.BlockSpec((1, ct, 1), lambda bi, i: (bi, i, 0)),
            _resident((1, LANES)),
            pl.BlockSpec((1, ct, NSA_Q), lambda bi, i: (bi, i, COL_NQ // NSA_Q)),
            col(kv_col + 2), col(kv_col + 3), col(kv_col + 4), col(kv_col + 5),
        ],
        out_specs=[
            pl.BlockSpec((1, NSA_Q, ct), lambda bi, i: (bi, 0, i)),
            pl.BlockSpec((1, ct, 2 * HEAD_DIM), lambda bi, i: (bi, i, 0)),
            vt_spec,
            pl.BlockSpec((1, ct, HEAD_DIM), lambda bi, i: (bi, i, 0)),
            vt_spec,
        ],
        out_shape=[
            jax.ShapeDtypeStruct((b, NSA_Q, s), BF16),
            jax.ShapeDtypeStruct((b, s, 2 * HEAD_DIM), BF16),
            vt_shape,
            jax.ShapeDtypeStruct((b, s, HEAD_DIM), BF16),
            vt_shape,
        ],
        compiler_params=_cparams("arbitrary", "arbitrary"),
        name="nsa_prep",
    )(posf, inv_row, proj3, proj3, proj3, proj3, proj3)


def _compress_body(kc_ref, vc_ref, pk_ref, pv_ref, kw1_ref, kw2_ref, vw1_ref, vw2_ref, posc_ref, inv_ref,
                   ko_ref, vo_ref):
    def comp(seg_ref, p_ref, w1_ref, w2_ref, transposed):
        seg = seg_ref[0]
        n_seg, half = seg.shape
        first = _dot(seg, w1_ref[:half, :])
        second = _dot(seg, w1_ref[half:, :])
        pos_term = _dot(jnp.broadcast_to(p_ref[...], (8, p_ref.shape[1])).astype(BF16), w1_ref[...])[0:1]
        hidden = first + pltpu.roll(second, n_seg - 1, 0) + pos_term
        act = _silu(hidden)
        if transposed:
            return _dot(w2_ref[...], act.T.astype(BF16))
        return _dot(act.astype(BF16), w2_ref[...])

    kc = comp(kc_ref, pk_ref, kw1_ref, kw2_ref, False)
    cos, sin_s = _rope_tables(posc_ref[0], inv_ref[...])
    ko_ref[0] = _rope(kc, cos, sin_s).astype(BF16)
    vo_ref[0] = comp(vc_ref, pv_ref, vw1_ref, vw2_ref, True).astype(BF16)


def _compress(kc_seg, vc_seg, pos_k, pos_v, kw1, kw2, vw1, vw2, posc, inv_row):
    b, n_seg, seg_w = kc_seg.shape
    seg_spec = pl.BlockSpec((1, n_seg, seg_w), lambda bi: (bi, 0, 0))
    return pl.pallas_call(
        _compress_body,
        grid=(b,),
        in_specs=[seg_spec, seg_spec, _resident(pos_k.shape), _resident(pos_v.shape), _resident(kw1.shape),
                  _resident(kw2.shape), _resident(vw1.shape), _resident(vw2.shape),
                  pl.BlockSpec((1, n_seg, 1), lambda bi: (bi, 0, 0)), _resident((1, LANES))],
        out_specs=[pl.BlockSpec((1, n_seg, HEAD_DIM), lambda bi: (bi, 0, 0)),
                   pl.BlockSpec((1, HEAD_DIM, n_seg), lambda bi: (bi, 0, 0))],
        out_shape=[jax.ShapeDtypeStruct((b, n_seg, HEAD_DIM), BF16),
                   jax.ShapeDtypeStruct((b, HEAD_DIM, n_seg), BF16)],
        compiler_params=_cparams("arbitrary"),
        name="compress",
    )(kc_seg, vc_seg, pos_k, pos_v, kw1, kw2, vw1, vw2, posc, inv_row)


def _split3(v):
    hi = v.astype(BF16)
    r1 = v - hi.astype(F32)
    mid = r1.astype(BF16)
    lo = (r1 - mid.astype(F32)).astype(BF16)
    return hi, mid, lo


def _nsa_attn_body(qt_ref, sm_ref, kc_ref, vct_ref, ks_ref, vst_ref, kw_ref, vwt_ref, o_ref,
                   m_sc, acc_sc, m2_sc, acc2_sc, s_buf, oc_sc, imp_sc, *, tq, n_cmp_pad, n_slc, top_k):
    qi = pl.program_id(1)
    t0 = qi * tq
    hq = NSA_HEADS * tq
    q_t = jnp.concatenate([qt_ref[0, h * HEAD_DIM:(h + 1) * HEAD_DIM, :] for h in range(NSA_HEADS)], axis=1)

    def lane_query(shape):
        return lax.broadcasted_iota(jnp.int32, shape, 1) & (tq - 1)

    def compressed_branch(rows):
        s_c = _dot(kc_ref[0, :rows, :], q_t)
        n_id = lax.broadcasted_iota(jnp.int32, (rows, hq), 0)
        valid_c = (n_id * CMP_STRIDE + (CMP_LEN - 1)) <= t0 + lane_query((rows, hq))
        m_c = jnp.max(jnp.where(valid_c, s_c, NEG), 0, keepdims=True)
        e_c = jnp.where(valid_c, jnp.exp2(s_c - m_c), 0.0)
        l_c = jnp.sum(e_c, 0, keepdims=True)
        p_c = e_c * (1.0 / jnp.where(l_c > 0.0, l_c, 1.0))
        oc_sc[...] = _dot(vct_ref[0, :, :rows], p_c.astype(BF16))
        p_sum = p_c[:, 0:tq]
        for h in range(1, NSA_HEADS):
            p_sum = p_sum + p_c[:, h * tq:(h + 1) * tq]
        oj = lax.broadcasted_iota(jnp.int32, (LANES, rows), 0) * SLC_BLOCK
        on = lax.broadcasted_iota(jnp.int32, (LANES, rows), 1) * CMP_STRIDE
        overlap_t = jnp.where((on < oj + SLC_BLOCK) & (on + CMP_LEN > oj), 1.0, 0.0).astype(BF16)
        imp_sc[...] = sum(_dot(overlap_t, part) for part in _split3(p_sum))

    n_groups = n_cmp_pad // LANES
    tokens_per_group = CMP_STRIDE * LANES
    need = ((qi + 1) * tq + tokens_per_group - 1) // tokens_per_group
    for g in range(1, n_groups + 1):
        pl.when((need == g) if g < n_groups else (need >= g))(functools.partial(compressed_branch, g * LANES))
    o_c = oc_sc[...]
    imp = imp_sc[...]

    j = lax.broadcasted_iota(jnp.int32, (LANES, tq), 0)
    cur = (t0 + lax.broadcasted_iota(jnp.int32, (LANES, tq), 1)) // SLC_BLOCK
    forced = (j == 0) | (j == cur) | (j == cur - 1)
    cand = jnp.where(jnp.logical_not(forced) & (j <= cur) & (j < n_slc), imp, -1.0)
    sel = jnp.where(forced, 1.0, 0.0)
    for _ in range(max(top_k - 3, 0)):
        best = jnp.max(cand, 0, keepdims=True)
        idx = jnp.min(jnp.where(cand == best, j, LANES), 0, keepdims=True)
        hit = j == idx
        sel = jnp.maximum(sel, jnp.where(hit, jnp.where(best >= 0.0, 1.0, 0.0), 0.0))
        cand = jnp.where(hit, -2.0, cand)
    bias = jnp.where(sel > 0.0, 0.0, NEG).astype(BF16)
    q_aug = jnp.concatenate([q_t, jnp.concatenate([bias] * NSA_HEADS, axis=1)], axis=0)

    def tile_scores(k_ref, n, q_op, mask):
        k_tile = k_ref[0, pl.ds(pl.multiple_of(n * tq, tq), tq), :]
        s = _dot(k_tile, q_op)
        return (s if mask is None else jnp.where(mask, s, NEG)).astype(BF16)

    def flash_update(m_ref, acc_ref, tiles):
        m_old = m_ref[...]
        m_loc = functools.reduce(jnp.maximum, [jnp.max(s, 0, keepdims=True) for s, _ in tiles])
        m_new = jnp.maximum(m_old, m_loc.astype(F32))
        m_b = m_new.astype(BF16)
        p = jnp.concatenate([jnp.exp2(s - m_b) for s, _ in tiles], axis=0)
        vt = jnp.concatenate([vt_tile for _, vt_tile in tiles], axis=1)
        acc_ref[...] = jnp.exp2(m_old - m_new) * acc_ref[...] + _dot(vt, p)
        m_ref[...] = m_new

    def flash_init(m_ref, acc_ref):
        m_ref[...] = jnp.full(m_ref.shape, NEG, F32)
        acc_ref[...] = jnp.zeros(acc_ref.shape, F32)

    kl = lax.broadcasted_iota(jnp.int32, (tq, hq), 0)
    ql = lane_query((tq, hq))
    causal = kl <= ql

    flash_init(m_sc, acc_sc)
    n_pairs = qi // 2

    def stage_scores(slot, pair):
        keys = ks_ref[0, pl.ds(pl.multiple_of(pair * (2 * tq), 2 * tq), 2 * tq), :]
        s_buf[slot] = _dot(keys, q_aug).astype(BF16)

    def consume(slot, pair):
        vt = jnp.concatenate([vst_ref[0, 2 * pair], vst_ref[0, 2 * pair + 1]], axis=1)
        flash_update(m_sc, acc_sc, [(s_buf[slot], vt)])

    @pl.when(n_pairs > 0)
    def _():
        stage_scores(0, 0)

    def two_pairs(j, carry):
        stage_scores(1, 2 * j + 1)
        consume(0, 2 * j)
        stage_scores(0, jnp.minimum(2 * j + 2, n_pairs - 1))
        consume(1, 2 * j + 1)
        return carry

    lax.fori_loop(0, n_pairs // 2, two_pairs, 0)

    @pl.when(n_pairs % 2 == 1)
    def _():
        consume(0, n_pairs - 1)

    odd = (qi % 2) == 1
    prev = jnp.maximum(qi - 1, 0)
    flash_update(m_sc, acc_sc, [
        (tile_scores(ks_ref, prev, q_aug, jnp.broadcast_to(odd, (tq, hq))), vst_ref[0, prev]),
        (tile_scores(ks_ref, qi, q_aug, causal), vst_ref[0, qi])])

    flash_init(m2_sc, acc2_sc)
    window_tiles = []
    for back in range(0, WINDOW // tq + 1):
        n = jnp.maximum(qi - back, 0)
        if back == 0:
            mask = causal
        elif back * tq == WINDOW:
            mask = (kl > ql) & (qi >= back)
        else:
            mask = jnp.broadcast_to(qi >= back, (tq, hq))
        window_tiles.append((tile_scores(kw_ref, n, q_t, mask), vwt_ref[0, n]))
    flash_update(m2_sc, acc2_sc, window_tiles)

    acc_s, acc_w = acc_sc[...], acc2_sc[...]
    o_s = acc_s[:HEAD_DIM] * (1.0 / acc_s[HEAD_DIM:HEAD_DIM + 1])
    o_w = acc_w[:HEAD_DIM] * (1.0 / acc_w[HEAD_DIM:HEAD_DIM + 1])
    gates_t = _sigmoid(sm_ref[0]).T
    eye = jnp.where(lax.broadcasted_iota(jnp.int32, (tq, tq), 0) == lax.broadcasted_iota(jnp.int32, (tq, tq), 1),
                    1.0, 0.0).astype(BF16)
    for h in range(NSA_HEADS):
        cols = slice(h * tq, (h + 1) * tq)
        c0 = SM_G + 3 * h
        o_t = (gates_t[c0:c0 + 1, :] * o_c[:, cols] + gates_t[c0 + 1:c0 + 2, :] * o_s[:, cols]
               + gates_t[c0 + 2:c0 + 3, :] * o_w[:, cols])
        o_ref[0, :, h * HEAD_DIM:(h + 1) * HEAD_DIM] = _dot_nt(eye, o_t.astype(BF16)).astype(o_ref.dtype)


def _nsa_attention(q_t, small3, k_cmp, v_cmp_t, ks_aug, vs_t, kw_rope, vw_t, tq):
    b, _, s = q_t.shape
    n_cmp_pad = k_cmp.shape[1]
    n_slc = s // SLC_BLOCK
    top_k = min(SLC_TOPK, n_slc)
    vrows = vs_t.shape[2]
    assert WINDOW % tq == 0 and vs_t.shape[-1] == tq
    batch_spec = lambda shape: pl.BlockSpec((1,) + shape, lambda bi, i: (bi,) + (0,) * len(shape))
    hq = NSA_HEADS * tq
    flash_scratch = [pltpu.VMEM((1, hq), F32), pltpu.VMEM((vrows, hq), F32)]
    return pl.pallas_call(
        functools.partial(_nsa_attn_body, tq=tq, n_cmp_pad=n_cmp_pad, n_slc=n_slc, top_k=top_k),
        grid=(b, s // tq),
        in_specs=[
            pl.BlockSpec((1, NSA_Q, tq), lambda bi, i: (bi, 0, i)),
            pl.BlockSpec((1, tq, LANES), lambda bi, i: (bi, i, 0)),
            batch_spec((n_cmp_pad, HEAD_DIM)),
            batch_spec((HEAD_DIM, n_cmp_pad)),
            batch_spec((s, 2 * HEAD_DIM)),
            batch_spec((s // tq, vrows, tq)),
            batch_spec((s, HEAD_DIM)),
            batch_spec((s // tq, vrows, tq)),
        ],
        out_specs=pl.BlockSpec((1, tq, NSA_Q), lambda bi, i: (bi, i, 0)),
        out_shape=jax.ShapeDtypeStruct((b, s, NSA_Q), BF16),
        scratch_shapes=flash_scratch + flash_scratch + [
            pltpu.VMEM((2, 2 * tq, hq), BF16),
            pltpu.VMEM((HEAD_DIM, hq), F32),
            pltpu.VMEM((LANES, tq), F32),
        ],
        compiler_params=_cparams("arbitrary", "arbitrary"),
        name="nsa_attn",
    )(q_t, small3, k_cmp, v_cmp_t, ks_aug, vs_t, kw_rope, vw_t)


def _gdn_body(xq_ref, xk_ref, xv_ref, z_ref, sm_ref, cw_ref, hp_ref, nw_ref, o_ref,
              xpad_sc, conv_sc, state_sc, qp_sc, op_sc, mp_sc, nn_sc, eg_sc, *, ct):
    i = pl.program_id(1)
    c = GDN_CHUNK
    nc = ct // c

    @pl.when(i == 0)
    def _():
        xpad_sc[:, 0:CONV_HIST, :] = jnp.zeros((3, CONV_HIST, GDN_QK), BF16)
        state_sc[...] = jnp.zeros_like(state_sc)

    row = lax.broadcasted_iota(jnp.int32, (ct, LANES), 0)
    row_in_chunk = row & (c - 1)
    ii = lax.broadcasted_iota(jnp.int32, (nc, c, c), 1)
    jj = lax.broadcasted_iota(jnp.int32, (nc, c, c), 2)

    blk = CONV_HIST
    for part, x_ref in enumerate((xq_ref, xk_ref, xv_ref)):
        wcols = slice(part * GDN_QK, (part + 1) * GDN_QK)
        xpad_sc[part, blk:, :] = x_ref[0]
        for r in range(ct // blk):
            window = xpad_sc[part, r * blk:(r + 2) * blk, :].astype(F32)
            acc = window[blk:] * cw_ref[GDN_CONV - 1:GDN_CONV, wcols]
            for k in range(1, GDN_CONV):
                acc = acc + window[blk - k:2 * blk - k] * cw_ref[GDN_CONV - 1 - k:GDN_CONV - k, wcols]
            conv_sc[part, r * blk:(r + 1) * blk, :] = _silu(acc)
        xpad_sc[part, 0:blk, :] = xpad_sc[part, ct:ct + blk, :]

    def conv_silu(part, h):
        return conv_sc[part, :, h * HEAD_DIM:(h + 1) * HEAD_DIM]

    def l2n(v):
        return v * lax.rsqrt(jnp.sum(v * v, -1, keepdims=True) + RMS_EPS)

    sm = sm_ref[0]
    sp_in = sm + hp_ref[1:2, :]
    softplus = jnp.maximum(sp_in, 0.0) + jnp.log(1.0 + jnp.exp(-jnp.abs(sp_in)))
    g_all = -jnp.exp(hp_ref[0:1, :]) * softplus
    beta_all = _sigmoid(sm)
    for h in range(GDN_HEADS):
        q = l2n(conv_silu(0, h)) * (HEAD_DIM ** -0.5)
        k = l2n(conv_silu(1, h))
        v = conv_silu(2, h)
        g = g_all[:, SM_A + h:SM_A + h + 1]
        beta = beta_all[:, SM_B + h:SM_B + h + 1]
        gc = jnp.broadcast_to(g, (ct, LANES))
        step = 1
        while step < c:
            gc = gc + jnp.where(row_in_chunk >= step, pltpu.roll(gc, step, 0), 0.0)
            step *= 2
        egc = jnp.exp(gc)
        kb = k * beta
        vb = v * beta
        gc3 = gc.reshape(nc, c, LANES)
        g_col = gc3[:, :, :c]
        g_row = jnp.swapaxes(gc3, 1, 2)[:, :c, :]
        decay = jnp.where(ii >= jj, jnp.exp(jnp.minimum(g_col - g_row, 0.0)), 0.0)
        k3 = k.reshape(nc, c, LANES).astype(BF16)
        kk = jnp.einsum("nid,njd->nij", kb.reshape(nc, c, LANES).astype(BF16), k3, preferred_element_type=F32)
        a_mat = jnp.where(ii > jj, kk * decay, 0.0)
        n_mat = -a_mat
        pw = a_mat
        sq = 2
        while sq < c:
            pwb = pw.astype(BF16)
            pw = jnp.einsum("nij,njk->nik", pwb, pwb, preferred_element_type=F32)
            n_mat = n_mat + pw + jnp.einsum("nij,njk->nik", n_mat.astype(BF16), pw.astype(BF16),
                                            preferred_element_type=F32)
            sq *= 2
        nb = n_mat.astype(BF16)
        vb3 = vb.reshape(nc, c, LANES)
        kbg3 = (kb * egc).reshape(nc, c, LANES)
        u = vb3 + jnp.einsum("nij,njd->nid", nb, vb3.astype(BF16), preferred_element_type=F32)
        w = kbg3 + jnp.einsum("nij,njd->nid", nb, kbg3.astype(BF16), preferred_element_type=F32)
        attn = jnp.einsum("nid,njd->nij", q.reshape(nc, c, LANES).astype(BF16), k3,
                          preferred_element_type=F32) * decay
        g_last = gc3[:, c - 1:c, :]
        wb = w.astype(BF16)
        ub = u.astype(BF16)
        ab = attn.astype(BF16)
        kt_t = jnp.swapaxes(k.reshape(nc, c, LANES) * jnp.exp(g_last - gc3), 1, 2).astype(BF16)
        qd3 = (q * egc).reshape(nc, c, LANES)
        qp = qd3 - jnp.einsum("nij,njd->nid", ab, wb, preferred_element_type=F32)
        op = jnp.einsum("nij,nje->nie", ab, ub, preferred_element_type=F32)
        qp_sc[h] = qp.reshape(ct, LANES).astype(BF16)
        op_sc[h] = op.reshape(ct, LANES)
        mp_sc[h] = jnp.einsum("ndi,nie->nde", kt_t, wb, preferred_element_type=F32).astype(BF16)
        nn_sc[h] = jnp.einsum("ndi,nie->nde", kt_t, ub, preferred_element_type=F32)
        eg_sc[h] = jnp.broadcast_to(jnp.exp(g_last), (nc, 8, LANES))

    nw = nw_ref[...]

    def chunk_step(n, carry):
        rows = pl.ds(pl.multiple_of(n * c, c), c)
        for h in range(GDN_HEADS):
            state = state_sc[h]
            sb = state.astype(BF16)
            o = _dot(qp_sc[h, rows, :], sb) + op_sc[h, rows, :]
            chunk_decay = eg_sc[h, n][0:1]
            state_sc[h] = state * chunk_decay - _dot(mp_sc[h, n], sb) + nn_sc[h, n]
            zf = z_ref[0, rows, h * HEAD_DIM:(h + 1) * HEAD_DIM].astype(F32)
            o = o * lax.rsqrt(jnp.mean(o * o, -1, keepdims=True) + RMS_EPS) * nw * _silu(zf)
            o_ref[0, rows, h * HEAD_DIM:(h + 1) * HEAD_DIM] = o.astype(o_ref.dtype)
        return carry

    lax.fori_loop(0, nc, chunk_step, 0)


def _gdn(proj3, small3, conv_w, hp, norm_w, ct):
    b, s, _ = proj3.shape
    cb = lambda col: col // GDN_QK
    blk = lambda col: pl.BlockSpec((1, ct, GDN_QK), lambda bi, i: (bi, i, cb(col)))
    hsh = (GDN_HEADS, ct, LANES)
    nc = ct // GDN_CHUNK
    return pl.pallas_call(
        functools.partial(_gdn_body, ct=ct),
        grid=(b, s // ct),
        in_specs=[blk(COL_GQ), blk(COL_GK), blk(COL_GV), blk(COL_Z),
                  pl.BlockSpec((1, ct, LANES), lambda bi, i: (bi, i, 0)),
                  _resident(conv_w.shape), _resident(hp.shape), _resident(norm_w.shape)],
        out_specs=pl.BlockSpec((1, ct, GDN_V), lambda bi, i: (bi, i, 0)),
        out_shape=jax.ShapeDtypeStruct((b, s, GDN_V), BF16),
        scratch_shapes=[
            pltpu.VMEM((3, ct + CONV_HIST, GDN_QK), BF16),
            pltpu.VMEM((3, ct, GDN_QK), F32),
            pltpu.VMEM((GDN_HEADS, HEAD_DIM, HEAD_DIM), F32),
            pltpu.VMEM(hsh, BF16),
            pltpu.VMEM(hsh, F32),
            pltpu.VMEM((GDN_HEADS, nc, HEAD_DIM, HEAD_DIM), BF16),
            pltpu.VMEM((GDN_HEADS, nc, HEAD_DIM, HEAD_DIM), F32),
            pltpu.VMEM((GDN_HEADS, nc, 8, LANES), F32),
        ],
        compiler_params=_cparams("arbitrary", "arbitrary"),
        name="gdn",
    )(proj3, proj3, proj3, proj3, small3, conv_w, hp, norm_w)


def _merge_body(x_ref, oa_ref, ob_ref, ga_ref, gb_ref, wa_ref, wb_ref, wm_ref, g_ref, b_ref, o_ref):
    ya = _dot(oa_ref[...], wa_ref[...])
    yb = _dot(ob_ref[...], wb_ref[...])
    y = _sigmoid(ga_ref[...].astype(F32)) * ya + _sigmoid(gb_ref[...].astype(F32)) * yb
    out = _dot(y.astype(BF16), wm_ref[...])
    o_ref[...] = _layer_norm(DN_ALPHA * x_ref[...] + out, g_ref[...], b_ref[...])


def _merge(x2d, oa, ob, proj2d, wa, wb, wm, g, bias, tm):
    t, d = x2d.shape
    row = lambda w, cbi=0: pl.BlockSpec((tm, w), lambda i: (i, cbi))
    return pl.pallas_call(
        _merge_body,
        grid=(t // tm,),
        in_specs=[row(d), row(GDN_V), row(NSA_Q), row(d, COL_GATE_A // d), row(d, COL_GATE_B // d),
                  _resident(wa.shape), _resident(wb.shape), _resident(wm.shape), _resident(g.shape),
                  _resident(bias.shape)],
        out_specs=row(d),
        out_shape=jax.ShapeDtypeStruct((t, d), F32),
        compiler_params=_cparams("arbitrary"),
        name="merge",
    )(x2d, oa, ob, proj2d, proj2d, wa, wb, wm, g, bias)


def _memkv_body(m_ref, wk_ref, wv_ref, k_ref, v_ref):
    m = m_ref[...].astype(BF16)
    k_ref[...] = _dot(m, wk_ref[...]).astype(BF16)
    v_ref[...] = _dot(m, wv_ref[...]).astype(BF16)


def _memkv(mem2d, wk, wv, tm):
    t, d = mem2d.shape
    row = pl.BlockSpec((tm, d), lambda i: (i, 0))
    return pl.pallas_call(
        _memkv_body,
        grid=(t // tm,),
        in_specs=[row, _resident(wk.shape), _resident(wv.shape)],
        out_specs=[row, row],
        out_shape=[jax.ShapeDtypeStruct((t, d), BF16)] * 2,
        compiler_params=_cparams("arbitrary"),
        name="memkv",
    )(mem2d, wk, wv)


def _xattn_body(x_ref, k_ref, v_ref, wq_ref, wo_ref, g_ref, b_ref, o_ref):
    x = x_ref[0]
    q = _dot(x.astype(BF16), wq_ref[...]).astype(BF16)
    heads = []
    for h in range(XA_HEADS):
        sl = slice(h * XA_HEAD_DIM, (h + 1) * XA_HEAD_DIM)
        s = _dot_nt(q[:, sl], k_ref[0, :, sl]) * (XA_HEAD_DIM ** -0.5)
        m = jnp.max(s, -1, keepdims=True)
        e = jnp.exp(s - m)
        p = e * (1.0 / jnp.sum(e, -1, keepdims=True))
        heads.append(_dot(p.astype(BF16), v_ref[0, :, sl]).astype(BF16))
    o = jnp.concatenate(heads, axis=1)
    out = _dot(o, wo_ref[...])
    o_ref[0] = _layer_norm(DN_ALPHA * x + out, g_ref[...], b_ref[...])


def _xattn(x3, k3, v3, wq, wo, g, bias, tm):
    b, s, d = x3.shape
    m = k3.shape[1]
    return pl.pallas_call(
        _xattn_body,
        grid=(b, s // tm),
        in_specs=[pl.BlockSpec((1, tm, d), lambda bi, i: (bi, i, 0)),
                  pl.BlockSpec((1, m, d), lambda bi, i: (bi, 0, 0)),
                  pl.BlockSpec((1, m, d), lambda bi, i: (bi, 0, 0)),
                  _resident(wq.shape), _resident(wo.shape), _resident(g.shape), _resident(bias.shape)],
        out_specs=pl.BlockSpec((1, tm, d), lambda bi, i: (bi, i, 0)),
        out_shape=jax.ShapeDtypeStruct((b, s, d), F32),
        compiler_params=_cparams("arbitrary", "arbitrary"),
        name="xattn",
    )(x3, k3, v3, wq, wo, g, bias)


def _ffn_body(x_ref, wg_ref, wu_ref, wd_ref, g_ref, b_ref, o_ref, *, chunk):
    x = x_ref[...]
    xb = x.astype(BF16)
    out = None
    for c in range(0, wg_ref.shape[1], chunk):
        hid = _silu(_dot(xb, wg_ref[:, c:c + chunk])) * _dot(xb, wu_ref[:, c:c + chunk])
        part = _dot(hid.astype(BF16), wd_ref[c:c + chunk, :])
        out = part if out is None else out + part
    o_ref[...] = _layer_norm(DN_ALPHA * x + out, g_ref[...], b_ref[...])


def _ffn(x2d, wg, wu, wd, g, bias, tm):
    t, d = x2d.shape
    dff = wg.shape[1]
    chunk = dff // 2 if (dff // 2) % LANES == 0 else dff
    row = pl.BlockSpec((tm, d), lambda i: (i, 0))
    return pl.pallas_call(
        functools.partial(_ffn_body, chunk=chunk),
        grid=(t // tm,),
        in_specs=[row, _resident(wg.shape), _resident(wu.shape), _resident(wd.shape), _resident(g.shape),
                  _resident(bias.shape)],
        out_specs=row,
        out_shape=jax.ShapeDtypeStruct((t, d), F32),
        compiler_params=_cparams("arbitrary"),
        name="ffn",
    )(x2d, wg, wu, wd, g, bias)


def _tile(n, pref):
    t = min(n, pref)
    assert n % t == 0, (n, pref)
    return t


def _layer(x, mem, positions, w_in, conv_w, a_log, dt_bias, gdn_norm_w, cmp_pos_k, cmp_pos_v, cmp_k_w1, cmp_k_w2,
           cmp_v_w1, cmp_v_w2, w_up_gdn, w_up_nsa, w_mix_out, ln1_g, ln1_b, xa_wq, xa_wk, xa_wv, xa_wo, ln2_g,
           ln2_b, ffn_w_gate, ffn_w_up, ffn_w_down, ln3_g, ln3_b):
    b, s, d = x.shape
    t = b * s
    assert d == D_MODEL and s % WINDOW == 0 and s // SLC_BLOCK <= LANES and s >= 2 * WINDOW

    o_qkv, o_z = 0, 3 * GDN_QK
    o_a = o_z + GDN_V
    o_b = o_a + GDN_HEADS
    o_qn = o_b + GDN_HEADS
    o_kv6 = o_qn + NSA_Q
    o_gn = o_kv6 + 6 * HEAD_DIM
    o_ga = o_gn + 3 * NSA_HEADS
    o_gb = o_ga + D_MODEL
    w_main = jnp.concatenate([w_in[:, o_ga:o_gb], w_in[:, o_gb:o_gb + D_MODEL], w_in[:, o_qkv:o_z], w_in[:, o_z:o_a],
                              w_in[:, o_qn:o_kv6], w_in[:, o_kv6:o_gn]], axis=1).astype(BF16)
    n_small = 2 * GDN_HEADS + 3 * NSA_HEADS
    w_small = jnp.concatenate([w_in[:, o_a:o_qn], w_in[:, o_gn:o_ga], jnp.zeros((d, LANES - n_small), w_in.dtype)],
                              axis=1).astype(BF16)

    x2d = x.reshape(t, d)
    proj2d, small2d = _project(x2d, w_main, w_small, _tile(t, 512))
    proj3 = proj2d.reshape(b, s, N_MAIN)
    small3 = small2d.reshape(b, s, LANES)

    hp = jnp.zeros((8, LANES), F32).at[0, :GDN_HEADS].set(a_log.astype(F32)).at[1, :GDN_HEADS].set(
        dt_bias.astype(F32))
    o_gdn = _gdn(proj3, small3, conv_w.astype(F32), hp, gdn_norm_w.reshape(1, HEAD_DIM).astype(F32), _tile(s, 512))

    half = HEAD_DIM // 2
    inv = ROPE_THETA ** (-jnp.arange(half, dtype=F32) / half)
    inv_row = jnp.concatenate([inv, inv]).reshape(1, HEAD_DIM)
    posf = positions.astype(F32)
    tq = _tile(s, 256)
    q_t, ks_aug, vs_t, kw_rope, vw_t = _nsa_prep(proj3, posf.reshape(b, s, 1), inv_row, _tile(s, 512), tq)

    n_seg = s // CMP_STRIDE
    seg_w = CMP_STRIDE * HEAD_DIM
    kc_seg = proj3[:, :, COL_KV6:COL_KV6 + HEAD_DIM].reshape(b, n_seg, seg_w)
    vc_seg = proj3[:, :, COL_KV6 + HEAD_DIM:COL_KV6 + 2 * HEAD_DIM].reshape(b, n_seg, seg_w)
    posc = posf[:, CMP_LEN - 1::CMP_STRIDE]
    posc = jnp.concatenate([posc, posc[:, -1:]], axis=1).reshape(b, n_seg, 1)
    k_cmp, v_cmp_t = _compress(kc_seg, vc_seg, cmp_pos_k.reshape(1, CMP_LEN * HEAD_DIM).astype(F32),
                             cmp_pos_v.reshape(1, CMP_LEN * HEAD_DIM).astype(F32), cmp_k_w1.astype(BF16),
                             cmp_k_w2.astype(BF16), cmp_v_w1.astype(BF16), cmp_v_w2.T.astype(BF16), posc, inv_row)
    o_nsa = _nsa_attention(q_t, small3, k_cmp, v_cmp_t, ks_aug, vs_t, kw_rope, vw_t, tq)

    vec = lambda p: p.reshape(1, d).astype(F32)
    x1 = _merge(x2d, o_gdn.reshape(t, GDN_V), o_nsa.reshape(t, NSA_Q), proj2d, w_up_gdn.astype(BF16),
                w_up_nsa.astype(BF16), w_mix_out.astype(BF16), vec(ln1_g), vec(ln1_b), _tile(t, 1024))
    mlen = mem.shape[1]
    mk, mv = _memkv(mem.reshape(b * mlen, d), xa_wk.astype(BF16), xa_wv.astype(BF16), _tile(b * mlen, 512))
    x2 = _xattn(x1.reshape(b, s, d), mk.reshape(b, mlen, d), mv.reshape(b, mlen, d), xa_wq.astype(BF16),
                xa_wo.astype(BF16), vec(ln2_g), vec(ln2_b), _tile(s, 1024))
    x3 = _ffn(x2.reshape(t, d), ffn_w_gate.astype(BF16), ffn_w_up.astype(BF16), ffn_w_down.astype(BF16),
              vec(ln3_g), vec(ln3_b), _tile(t, 512))
    return x3.reshape(b, s, d)


def kernel(x, mem, positions, w_in, gdn_conv_w, gdn_a_log, gdn_dt_bias, gdn_norm_w, cmp_pos_k, cmp_pos_v, cmp_k_w1, cmp_k_w2, cmp_v_w1, cmp_v_w2, w_up_gdn, w_up_nsa, w_mix_out, ln1_g, ln1_b, xa_wq, xa_wk, xa_wv, xa_wo, ln2_g, ln2_b, ffn_w_gate, ffn_w_up, ffn_w_down, ln3_g, ln3_b):
    for l in range(w_in.shape[0]):
        x = _layer(x, mem, positions, w_in[l], gdn_conv_w[l], gdn_a_log[l], gdn_dt_bias[l], gdn_norm_w[l],
                   cmp_pos_k[l], cmp_pos_v[l], cmp_k_w1[l], cmp_k_w2[l], cmp_v_w1[l], cmp_v_w2[l], w_up_gdn[l],
                   w_up_nsa[l], w_mix_out[l], ln1_g[l], ln1_b[l], xa_wq[l], xa_wk[l], xa_wv[l], xa_wo[l], ln2_g[l],
                   ln2_b[l], ffn_w_gate[l], ffn_w_up[l], ffn_w_down[l], ln3_g[l], ln3_b[l])
    return x
```

```python
import functools

import jax
import jax.numpy as jnp
from jax import lax
from jax.experimental import pallas as pl
from jax.experimental.pallas import tpu as pltpu

F32 = jnp.float32
BF16 = jnp.bfloat16

D_MODEL = 1024
HEAD_DIM = 128
GDN_HEADS = 4
GDN_QK = GDN_HEADS * HEAD_DIM
GDN_V = GDN_HEADS * HEAD_DIM
GDN_CONV = 4
GDN_CHUNK = 64
NSA_HEADS = 4
NSA_Q = NSA_HEADS * HEAD_DIM
CMP_LEN = 32
CMP_STRIDE = 16
SLC_BLOCK = 64
SLC_TOPK = 16
WINDOW = 512
FORCE_SCORE = 1.0e4
XA_HEADS = 4
XA_HEAD_DIM = D_MODEL // XA_HEADS
ROPE_THETA = 10000.0
LN_EPS = 1e-5
RMS_EPS = 1e-6
NEG = -1.0e30
ATTN_SCALE = HEAD_DIM ** -0.5
LOG2E = 1.4426950408889634
DEPTH = 1
DN_ALPHA = (2 * DEPTH) ** 0.25

LANES = 128
CONV_HIST = 128
VT_ONES_ROWS = 16
VMEM_LIMIT_BYTES = 56 * 1024 * 1024

COL_GATE_A = 0
COL_GATE_B = 1024
COL_GQ = 2048
COL_GK = 2560
COL_GV = 3072
COL_Z = 3584
COL_NQ = 4096
COL_KV6 = 4608
N_MAIN = 5376
SM_A = 0
SM_B = 4
SM_G = 8


def _cparams(*sem):
    return pltpu.CompilerParams(dimension_semantics=sem, vmem_limit_bytes=VMEM_LIMIT_BYTES)


def _resident(shape):
    nd = len(shape)
    return pl.BlockSpec(shape, lambda *_: (0,) * nd, pipeline_mode=pl.Buffered(1))


def _layer_norm(v, g, b):
    mu = jnp.mean(v, -1, keepdims=True)
    c = v - mu
    var = jnp.mean(c * c, -1, keepdims=True)
    return c * lax.rsqrt(var + LN_EPS) * g + b


def _silu(v):
    half = 0.5 * v
    return half + half * jnp.tanh(half)


def _sigmoid(v):
    return 0.5 + 0.5 * jnp.tanh(0.5 * v)


def _dot(a, b):
    return jnp.dot(a, b, preferred_element_type=F32)


def _dot_nt(a, b):
    return lax.dot_general(a, b, (((1,), (1,)), ((), ())), preferred_element_type=F32)


def _proj_body(x_ref, w_ref, ws_ref, o_ref, os_ref, *, chunk):
    x = x_ref[...].astype(BF16)
    for c in range(0, w_ref.shape[1], chunk):
        o_ref[:, c:c + chunk] = _dot(x, w_ref[:, c:c + chunk]).astype(o_ref.dtype)
    os_ref[...] = _dot(x, ws_ref[...])


def _project(x2d, w_main, w_small, tm):
    t, d = x2d.shape
    n = w_main.shape[1]
    return pl.pallas_call(
        functools.partial(_proj_body, chunk=768),
        grid=(t // tm,),
        in_specs=[pl.BlockSpec((tm, d), lambda i: (i, 0)), _resident((d, n)), _resident((d, LANES))],
        out_specs=[pl.BlockSpec((tm, n), lambda i: (i, 0)), pl.BlockSpec((tm, LANES), lambda i: (i, 0))],
        out_shape=[jax.ShapeDtypeStruct((t, n), BF16), jax.ShapeDtypeStruct((t, LANES), F32)],
        compiler_params=_cparams("arbitrary"),
        name="proj",
    )(x2d, w_main, w_small)


def _rope_tables(pos_col, inv_row):
    n = pos_col.shape[0]
    half_n, half_d = n // 2, HEAD_DIM // 2
    low = lax.broadcasted_iota(jnp.int32, (half_n, HEAD_DIM), 1) < half_d
    ang = jnp.where(low, pos_col[:half_n], pos_col[half_n:]) * inv_row
    cos = jnp.cos(ang)
    sin = jnp.sin(ang)
    cos_sw = pltpu.roll(cos, half_d, 1)
    sin_sw = pltpu.roll(sin, half_d, 1)
    cos_full = jnp.concatenate([jnp.where(low, cos, cos_sw), jnp.where(low, cos_sw, cos)], axis=0)
    sin_signed = jnp.concatenate([jnp.where(low, -sin, sin_sw), jnp.where(low, -sin_sw, sin)], axis=0)
    return cos_full, sin_signed


def _rope(v, cos, sin_signed):
    return v * cos + pltpu.roll(v, HEAD_DIM // 2, 1) * sin_signed


def _nsa_prep_body(pos_ref, inv_ref, q_ref, ks_ref, vs_ref, kw_ref, vw_ref,
                   qt_ref, kso_ref, vst_ref, kwo_ref, vwt_ref, *, ct, tk):
    i = pl.program_id(1)
    cos, sin_s = _rope_tables(pos_ref[0], inv_ref[...])
    for h in range(NSA_HEADS):
        sl = slice(h * HEAD_DIM, (h + 1) * HEAD_DIM)
        q_rot = _rope(q_ref[0, :, sl].astype(F32), cos, sin_s) * (ATTN_SCALE * LOG2E)
        qt_ref[0, sl, :] = q_rot.T.astype(BF16)
    kso_ref[0, :, :HEAD_DIM] = _rope(ks_ref[0].astype(F32), cos, sin_s).astype(BF16)
    lane = lax.broadcasted_iota(jnp.int32, (ct, LANES), 1)
    tok = lax.broadcasted_iota(jnp.int32, (ct, LANES), 0) + i * ct
    kso_ref[0, :, HEAD_DIM:] = jnp.where(lane == tok // SLC_BLOCK, 1.0, 0.0).astype(BF16)
    kwo_ref[0] = _rope(kw_ref[0].astype(F32), cos, sin_s).astype(BF16)
    ones = jnp.ones((VT_ONES_ROWS, tk), BF16)
    for v_ref, vt_ref in ((vs_ref, vst_ref), (vw_ref, vwt_ref)):
        v_t = v_ref[0].astype(F32).T
        for n in range(ct // tk):
            vt_ref[0, n, :HEAD_DIM, :] = v_t[:, n * tk:(n + 1) * tk].astype(BF16)
            vt_ref[0, n, HEAD_DIM:, :] = ones


def _nsa_prep(proj3, posf, inv_row, ct, tk):
    b, s, _ = proj3.shape
    kv_col = COL_KV6 // HEAD_DIM
    col = lambda c: pl.BlockSpec((1, ct, HEAD_DIM), lambda bi, i: (bi, i, c))
    vrows = HEAD_DIM + VT_ONES_ROWS
    vt_spec = pl.BlockSpec((1, ct // tk, vrows, tk), lambda bi, i: (bi, i, 0, 0))
    vt_shape = jax.ShapeDtypeStruct((b, s // tk, vrows, tk), BF16)
    return pl.pallas_call(
        functools.partial(_nsa_prep_body, ct=ct, tk=tk),
        grid=(b, s // ct),
        in_specs=[
            pl.BlockSpec((1, ct, 1), lambda bi, i: (bi, i, 0)),
            _resident((1, LANES)),
            pl.BlockSpec((1, ct, NSA_Q), lambda bi, i: (bi, i, COL_NQ // NSA_Q)),
            col(kv_col + 2), col(kv_col + 3), col(kv_col + 4), col(kv_col + 5),
        ],
        out_specs=[
            pl.BlockSpec((1, NSA_Q, ct), lambda bi, i: (bi, 0, i)),
            pl.BlockSpec((1, ct, 2 * HEAD_DIM), lambda bi, i: (bi, i, 0)),
            vt_spec,
            pl.BlockSpec((1, ct, HEAD_DIM), lambda bi, i: (bi, i, 0)),
            vt_spec,
        ],
        out_shape=[
            jax.ShapeDtypeStruct((b, NSA_Q, s), BF16),
            jax.ShapeDtypeStruct((b, s, 2 * HEAD_DIM), BF16),
            vt_shape,
            jax.ShapeDtypeStruct((b, s, HEAD_DIM), BF16),
            vt_shape,
        ],
        compiler_params=_cparams("arbitrary", "arbitrary"),
        name="nsa_prep",
    )(posf, inv_row, proj3, proj3, proj3, proj3, proj3)


def _compress_body(kc_ref, vc_ref, pk_ref, pv_ref, kw1_ref, kw2_ref, vw1_ref, vw2_ref, posc_ref, inv_ref,
                   ko_ref, vo_ref):
    def comp(seg_ref, p_ref, w1_ref, w2_ref, transposed):
        seg = seg_ref[0]
        n_seg, half = seg.shape
        first = _dot(seg, w1_ref[:half, :])
        second = _dot(seg, w1_ref[half:, :])
        pos_term = _dot(jnp.broadcast_to(p_ref[...], (8, p_ref.shape[1])).astype(BF16), w1_ref[...])[0:1]
        hidden = first + pltpu.roll(second, n_seg - 1, 0) + pos_term
        act = _silu(hidden)
        if transposed:
            return _dot(w2_ref[...], act.T.astype(BF16))
        return _dot(act.astype(BF16), w2_ref[...])

    kc = comp(kc_ref, pk_ref, kw1_ref, kw2_ref, False)
    cos, sin_s = _rope_tables(posc_ref[0], inv_ref[...])
    ko_ref[0] = _rope(kc, cos, sin_s).astype(BF16)
    vo_ref[0] = comp(vc_ref, pv_ref, vw1_ref, vw2_ref, True).astype(BF16)


def _compress(kc_seg, vc_seg, pos_k, pos_v, kw1, kw2, vw1, vw2, posc, inv_row):
    b, n_seg, seg_w = kc_seg.shape
    seg_spec = pl.BlockSpec((1, n_seg, seg_w), lambda bi: (bi, 0, 0))
    return pl.pallas_call(
        _compress_body,
        grid=(b,),
        in_specs=[seg_spec, seg_spec, _resident(pos_k.shape), _resident(pos_v.shape), _resident(kw1.shape),
                  _resident(kw2.shape), _resident(vw1.shape), _resident(vw2.shape),
                  pl.BlockSpec((1, n_seg, 1), lambda bi: (bi, 0, 0)), _resident((1, LANES))],
        out_specs=[pl.BlockSpec((1, n_seg, HEAD_DIM), lambda bi: (bi, 0, 0)),
                   pl.BlockSpec((1, HEAD_DIM, n_seg), lambda bi: (bi, 0, 0))],
        out_shape=[jax.ShapeDtypeStruct((b, n_seg, HEAD_DIM), BF16),
                   jax.ShapeDtypeStruct((b, HEAD_DIM, n_seg), BF16)],
        compiler_params=_cparams("arbitrary"),
        name="compress",
    )(kc_seg, vc_seg, pos_k, pos_v, kw1, kw2, vw1, vw2, posc, inv_row)


def _split3(v):
    hi = v.astype(BF16)
    r1 = v - hi.astype(F32)
    mid = r1.astype(BF16)
    lo = (r1 - mid.astype(F32)).astype(BF16)
    return hi, mid, lo


def _nsa_attn_body(qt_ref, sm_ref, kc_ref, vct_ref, ks_ref, vst_ref, kw_ref, vwt_ref, o_ref,
                   m_sc, acc_sc, m2_sc, acc2_sc, s_buf, oc_sc, imp_sc, *, tq, n_cmp_pad, n_slc, top_k):
    qi = pl.program_id(1)
    t0 = qi * tq
    hq = NSA_HEADS * tq
    q_t = jnp.concatenate([qt_ref[0, h * HEAD_DIM:(h + 1) * HEAD_DIM, :] for h in range(NSA_HEADS)], axis=1)

    def lane_query(shape):
        return lax.broadcasted_iota(jnp.int32, shape, 1) & (tq - 1)

    def compressed_branch(rows):
        s_c = _dot(kc_ref[0, :rows, :], q_t)
        n_id = lax.broadcasted_iota(jnp.int32, (rows, hq), 0)
        valid_c = (n_id * CMP_STRIDE + (CMP_LEN - 1)) <= t0 + lane_query((rows, hq))
        m_c = jnp.max(jnp.where(valid_c, s_c, NEG), 0, keepdims=True)
        e_c = jnp.where(valid_c, jnp.exp2(s_c - m_c), 0.0)
        l_c = jnp.sum(e_c, 0, keepdims=True)
        p_c = e_c * (1.0 / jnp.where(l_c > 0.0, l_c, 1.0))
        oc_sc[...] = _dot(vct_ref[0, :, :rows], p_c.astype(BF16))
        p_sum = p_c[:, 0:tq]
        for h in range(1, NSA_HEADS):
            p_sum = p_sum + p_c[:, h * tq:(h + 1) * tq]
        oj = lax.broadcasted_iota(jnp.int32, (LANES, rows), 0) * SLC_BLOCK
        on = lax.broadcasted_iota(jnp.int32, (LANES, rows), 1) * CMP_STRIDE
        overlap_t = jnp.where((on < oj + SLC_BLOCK) & (on + CMP_LEN > oj), 1.0, 0.0).astype(BF16)
        imp_sc[...] = sum(_dot(overlap_t, part) for part in _split3(p_sum))

    n_groups = n_cmp_pad // LANES
    tokens_per_group = CMP_STRIDE * LANES
    need = ((qi + 1) * tq + tokens_per_group - 1) // tokens_per_group
    for g in range(1, n_groups + 1):
        pl.when((need == g) if g < n_groups else (need >= g))(functools.partial(compressed_branch, g * LANES))
    o_c = oc_sc[...]
    imp = imp_sc[...]

    j = lax.broadcasted_iota(jnp.int32, (LANES, tq), 0)
    cur = (t0 + lax.broadcasted_iota(jnp.int32, (LANES, tq), 1)) // SLC_BLOCK
    forced = (j == 0) | (j == cur) | (j == cur - 1)
    cand = jnp.where(jnp.logical_not(forced) & (j <= cur) & (j < n_slc), imp, -1.0)
    sel = jnp.where(forced, 1.0, 0.0)
    for _ in range(max(top_k - 3, 0)):
        best = jnp.max(cand, 0, keepdims=True)
        idx = jnp.min(jnp.where(cand == best, j, LANES), 0, keepdims=True)
        hit = j == idx
        sel = jnp.maximum(sel, jnp.where(hit, jnp.where(best >= 0.0, 1.0, 0.0), 0.0))
        cand = jnp.where(hit, -2.0, cand)
    bias = jnp.where(sel > 0.0, 0.0, NEG).astype(BF16)
    q_aug = jnp.concatenate([q_t, jnp.concatenate([bias] * NSA_HEADS, axis=1)], axis=0)

    def tile_scores(k_ref, n, q_op, mask):
        k_tile = k_ref[0, pl.ds(pl.multiple_of(n * tq, tq), tq), :]
        s = _dot(k_tile, q_op)
        return (s if mask is None else jnp.where(mask, s, NEG)).astype(BF16)

    def flash_update(m_ref, acc_ref, tiles):
        m_old = m_ref[...]
        m_loc = functools.reduce(jnp.maximum, [jnp.max(s, 0, keepdims=True) for s, _ in tiles])
        m_new = jnp.maximum(m_old, m_loc.astype(F32))
        m_b = m_new.astype(BF16)
        p = jnp.concatenate([jnp.exp2(s - m_b) for s, _ in tiles], axis=0)
        vt = jnp.concatenate([vt_tile for _, vt_tile in tiles], axis=1)
        acc_ref[...] = jnp.exp2(m_old - m_new) * acc_ref[...] + _dot(vt, p)
        m_ref[...] = m_new

    def flash_init(m_ref, acc_ref):
        m_ref[...] = jnp.full(m_ref.shape, NEG, F32)
        acc_ref[...] = jnp.zeros(acc_ref.shape, F32)

    kl = lax.broadcasted_iota(jnp.int32, (tq, hq), 0)
    ql = lane_query((tq, hq))
    causal = kl <= ql

    flash_init(m_sc, acc_sc)
    n_pairs = qi // 2

    def stage_scores(slot, pair):
        keys = ks_ref[0, pl.ds(pl.multiple_of(pair * (2 * tq), 2 * tq), 2 * tq), :]
        s_buf[slot] = _dot(keys, q_aug).astype(BF16)

    def consume(slot, pair):
        vt = jnp.concatenate([vst_ref[0, 2 * pair], vst_ref[0, 2 * pair + 1]], axis=1)
        flash_update(m_sc, acc_sc, [(s_buf[slot], vt)])

    @pl.when(n_pairs > 0)
    def _():
        stage_scores(0, 0)

    def two_pairs(j, carry):
        stage_scores(1, 2 * j + 1)
        consume(0, 2 * j)
        stage_scores(0, jnp.minimum(2 * j + 2, n_pairs - 1))
        consume(1, 2 * j + 1)
        return carry

    lax.fori_loop(0, n_pairs // 2, two_pairs, 0)

    @pl.when(n_pairs % 2 == 1)
    def _():
        consume(0, n_pairs - 1)

    odd = (qi % 2) == 1
    prev = jnp.maximum(qi - 1, 0)
    flash_update(m_sc, acc_sc, [
        (tile_scores(ks_ref, prev, q_aug, jnp.broadcast_to(odd, (tq, hq))), vst_ref[0, prev]),
        (tile_scores(ks_ref, qi, q_aug, causal), vst_ref[0, qi])])

    flash_init(m2_sc, acc2_sc)
    window_tiles = []
    for back in range(0, WINDOW // tq + 1):
        n = jnp.maximum(qi - back, 0)
        if back == 0:
            mask = causal
        elif back * tq == WINDOW:
            mask = (kl > ql) & (qi >= back)
        else:
            mask = jnp.broadcast_to(qi >= back, (tq, hq))
        window_tiles.append((tile_scores(kw_ref, n, q_t, mask), vwt_ref[0, n]))
    flash_update(m2_sc, acc2_sc, window_tiles)

    acc_s, acc_w = acc_sc[...], acc2_sc[...]
    o_s = acc_s[:HEAD_DIM] * (1.0 / acc_s[HEAD_DIM:HEAD_DIM + 1])
    o_w = acc_w[:HEAD_DIM] * (1.0 / acc_w[HEAD_DIM:HEAD_DIM + 1])
    gates_t = _sigmoid(sm_ref[0]).T
    eye = jnp.where(lax.broadcasted_iota(jnp.int32, (tq, tq), 0) == lax.broadcasted_iota(jnp.int32, (tq, tq), 1),
                    1.0, 0.0).astype(BF16)
    for h in range(NSA_HEADS):
        cols = slice(h * tq, (h + 1) * tq)
        c0 = SM_G + 3 * h
        o_t = (gates_t[c0:c0 + 1, :] * o_c[:, cols] + gates_t[c0 + 1:c0 + 2, :] * o_s[:, cols]
               + gates_t[c0 + 2:c0 + 3, :] * o_w[:, cols])
        o_ref[0, :, h * HEAD_DIM:(h + 1) * HEAD_DIM] = _dot_nt(eye, o_t.astype(BF16)).astype(o_ref.dtype)


def _nsa_attention(q_t, small3, k_cmp, v_cmp_t, ks_aug, vs_t, kw_rope, vw_t, tq):
    b, _, s = q_t.shape
    n_cmp_pad = k_cmp.shape[1]
    n_slc = s // SLC_BLOCK
    top_k = min(SLC_TOPK, n_slc)
    vrows = vs_t.shape[2]
    assert WINDOW % tq == 0 and vs_t.shape[-1] == tq
    batch_spec = lambda shape: pl.BlockSpec((1,) + shape, lambda bi, i: (bi,) + (0,) * len(shape))
    hq = NSA_HEADS * tq
    flash_scratch = [pltpu.VMEM((1, hq), F32), pltpu.VMEM((vrows, hq), F32)]
    return pl.pallas_call(
        functools.partial(_nsa_attn_body, tq=tq, n_cmp_pad=n_cmp_pad, n_slc=n_slc, top_k=top_k),
        grid=(b, s // tq),
        in_specs=[
            pl.BlockSpec((1, NSA_Q, tq), lambda bi, i: (bi, 0, i)),
            pl.BlockSpec((1, tq, LANES), lambda bi, i: (bi, i, 0)),
            batch_spec((n_cmp_pad, HEAD_DIM)),
            batch_spec((HEAD_DIM, n_cmp_pad)),
            batch_spec((s, 2 * HEAD_DIM)),
            batch_spec((s // tq, vrows, tq)),
            batch_spec((s, HEAD_DIM)),
            batch_spec((s // tq, vrows, tq)),
        ],
        out_specs=pl.BlockSpec((1, tq, NSA_Q), lambda bi, i: (bi, i, 0)),
        out_shape=jax.ShapeDtypeStruct((b, s, NSA_Q), BF16),
        scratch_shapes=flash_scratch + flash_scratch + [
            pltpu.VMEM((2, 2 * tq, hq), BF16),
            pltpu.VMEM((HEAD_DIM, hq), F32),
            pltpu.VMEM((LANES, tq), F32),
        ],
        compiler_params=_cparams("arbitrary", "arbitrary"),
        name="nsa_attn",
    )(q_t, small3, k_cmp, v_cmp_t, ks_aug, vs_t, kw_rope, vw_t)


def _gdn_body(xq_ref, xk_ref, xv_ref, z_ref, sm_ref, cw_ref, hp_ref, nw_ref, o_ref,
              xpad_sc, conv_sc, state_sc, qp_sc, op_sc, mp_sc, nn_sc, eg_sc, *, ct):
    i = pl.program_id(1)
    c = GDN_CHUNK
    nc = ct // c

    @pl.when(i == 0)
    def _():
        xpad_sc[:, 0:CONV_HIST, :] = jnp.zeros((3, CONV_HIST, GDN_QK), BF16)
        state_sc[...] = jnp.zeros_like(state_sc)

    row = lax.broadcasted_iota(jnp.int32, (ct, LANES), 0)
    row_in_chunk = row & (c - 1)
    ii = lax.broadcasted_iota(jnp.int32, (nc, c, c), 1)
    jj = lax.broadcasted_iota(jnp.int32, (nc, c, c), 2)

    blk = CONV_HIST
    for part, x_ref in enumerate((xq_ref, xk_ref, xv_ref)):
        wcols = slice(part * GDN_QK, (part + 1) * GDN_QK)
        xpad_sc[part, blk:, :] = x_ref[0]
        for r in range(ct // blk):
            window = xpad_sc[part, r * blk:(r + 2) * blk, :].astype(F32)
            acc = window[blk:] * cw_ref[GDN_CONV - 1:GDN_CONV, wcols]
            for k in range(1, GDN_CONV):
                acc = acc + window[blk - k:2 * blk - k] * cw_ref[GDN_CONV - 1 - k:GDN_CONV - k, wcols]
            conv_sc[part, r * blk:(r + 1) * blk, :] = _silu(acc)
        xpad_sc[part, 0:blk, :] = xpad_sc[part, ct:ct + blk, :]

    def conv_silu(part, h):
        return conv_sc[part, :, h * HEAD_DIM:(h + 1) * HEAD_DIM]

    def l2n(v):
        return v * lax.rsqrt(jnp.sum(v * v, -1, keepdims=True) + RMS_EPS)

    sm = sm_ref[0]
    sp_in = sm + hp_ref[1:2, :]
    softplus = jnp.maximum(sp_in, 0.0) + jnp.log(1.0 + jnp.exp(-jnp.abs(sp_in)))
    g_all = -jnp.exp(hp_ref[0:1, :]) * softplus
    beta_all = _sigmoid(sm)
    for h in range(GDN_HEADS):
        q = l2n(conv_silu(0, h)) * (HEAD_DIM ** -0.5)
        k = l2n(conv_silu(1, h))
        v = conv_silu(2, h)
        g = g_all[:, SM_A + h:SM_A + h + 1]
        beta = beta_all[:, SM_B + h:SM_B + h + 1]
        gc = jnp.broadcast_to(g, (ct, LANES))
        step = 1
        while step < c:
            gc = gc + jnp.where(row_in_chunk >= step, pltpu.roll(gc, step, 0), 0.0)
            step *= 2
        egc = jnp.exp(gc)
        kb = k * beta
        vb = v * beta
        gc3 = gc.reshape(nc, c, LANES)
        g_col = gc3[:, :, :c]
        g_row = jnp.swapaxes(gc3, 1, 2)[:, :c, :]
        decay = jnp.where(ii >= jj, jnp.exp(jnp.minimum(g_col - g_row, 0.0)), 0.0)
        k3 = k.reshape(nc, c, LANES).astype(BF16)
        kk = jnp.einsum("nid,njd->nij", kb.reshape(nc, c, LANES).astype(BF16), k3, preferred_element_type=F32)
        a_mat = jnp.where(ii > jj, kk * decay, 0.0)
        n_mat = -a_mat
        pw = a_mat
        sq = 2
        while sq < c:
            pwb = pw.astype(BF16)
            pw = jnp.einsum("nij,njk->nik", pwb, pwb, preferred_element_type=F32)
            n_mat = n_mat + pw + jnp.einsum("nij,njk->nik", n_mat.astype(BF16), pw.astype(BF16),
                                            preferred_element_type=F32)
            sq *= 2
        nb = n_mat.astype(BF16)
        vb3 = vb.reshape(nc, c, LANES)
        kbg3 = (kb * egc).reshape(nc, c, LANES)
        u = vb3 + jnp.einsum("nij,njd->nid", nb, vb3.astype(BF16), preferred_element_type=F32)
        w = kbg3 + jnp.einsum("nij,njd->nid", nb, kbg3.astype(BF16), preferred_element_type=F32)
        attn = jnp.einsum("nid,njd->nij", q.reshape(nc, c, LANES).astype(BF16), k3,
                          preferred_element_type=F32) * decay
        g_last = gc3[:, c - 1:c, :]
        wb = w.astype(BF16)
        ub = u.astype(BF16)
        ab = attn.astype(BF16)
        kt_t = jnp.swapaxes(k.reshape(nc, c, LANES) * jnp.exp(g_last - gc3), 1, 2).astype(BF16)
        qd3 = (q * egc).reshape(nc, c, LANES)
        qp = qd3 - jnp.einsum("nij,njd->nid", ab, wb, preferred_element_type=F32)
        op = jnp.einsum("nij,nje->nie", ab, ub, preferred_element_type=F32)
        qp_sc[h] = qp.reshape(ct, LANES).astype(BF16)
        op_sc[h] = op.reshape(ct, LANES)
        mp_sc[h] = jnp.einsum("ndi,nie->nde", kt_t, wb, preferred_element_type=F32).astype(BF16)
        nn_sc[h] = jnp.einsum("ndi,nie->nde", kt_t, ub, preferred_element_type=F32)
        eg_sc[h] = jnp.broadcast_to(jnp.exp(g_last), (nc, 8, LANES))

    nw = nw_ref[...]

    def chunk_step(n, carry):
        rows = pl.ds(pl.multiple_of(n * c, c), c)
        for h in range(GDN_HEADS):
            state = state_sc[h]
            sb = state.astype(BF16)
            o = _dot(qp_sc[h, rows, :], sb) + op_sc[h, rows, :]
            chunk_decay = eg_sc[h, n][0:1]
            state_sc[h] = state * chunk_decay - _dot(mp_sc[h, n], sb) + nn_sc[h, n]
            zf = z_ref[0, rows, h * HEAD_DIM:(h + 1) * HEAD_DIM].astype(F32)
            o = o * lax.rsqrt(jnp.mean(o * o, -1, keepdims=True) + RMS_EPS) * nw * _silu(zf)
            o_ref[0, rows, h * HEAD_DIM:(h + 1) * HEAD_DIM] = o.astype(o_ref.dtype)
        return carry

    lax.fori_loop(0, nc, chunk_step, 0)


def _gdn(proj3, small3, conv_w, hp, norm_w, ct):
    b, s, _ = proj3.shape
    cb = lambda col: col // GDN_QK
    blk = lambda col: pl.BlockSpec((1, ct, GDN_QK), lambda bi, i: (bi, i, cb(col)))
    hsh = (GDN_HEADS, ct, LANES)
    nc = ct // GDN_CHUNK
    return pl.pallas_call(
        functools.partial(_gdn_body, ct=ct),
        grid=(b, s // ct),
        in_specs=[blk(COL_GQ), blk(COL_GK), blk(COL_GV), blk(COL_Z),
                  pl.BlockSpec((1, ct, LANES), lambda bi, i: (bi, i, 0)),
                  _resident(conv_w.shape), _resident(hp.shape), _resident(norm_w.shape)],
        out_specs=pl.BlockSpec((1, ct, GDN_V), lambda bi, i: (bi, i, 0)),
        out_shape=jax.ShapeDtypeStruct((b, s, GDN_V), BF16),
        scratch_shapes=[
            pltpu.VMEM((3, ct + CONV_HIST, GDN_QK), BF16),
            pltpu.VMEM((3, ct, GDN_QK), F32),
            pltpu.VMEM((GDN_HEADS, HEAD_DIM, HEAD_DIM), F32),
            pltpu.VMEM(hsh, BF16),
            pltpu.VMEM(hsh, F32),
            pltpu.VMEM((GDN_HEADS, nc, HEAD_DIM, HEAD_DIM), BF16),
            pltpu.VMEM((GDN_HEADS, nc, HEAD_DIM, HEAD_DIM), F32),
            pltpu.VMEM((GDN_HEADS, nc, 8, LANES), F32),
        ],
        compiler_params=_cparams("arbitrary", "arbitrary"),
        name="gdn",
    )(proj3, proj3, proj3, proj3, small3, conv_w, hp, norm_w)


def _merge_body(x_ref, oa_ref, ob_ref, ga_ref, gb_ref, wa_ref, wb_ref, wm_ref, g_ref, b_ref, o_ref):
    ya = _dot(oa_ref[...], wa_ref[...])
    yb = _dot(ob_ref[...], wb_ref[...])
    y = _sigmoid(ga_ref[...].astype(F32)) * ya + _sigmoid(gb_ref[...].astype(F32)) * yb
    out = _dot(y.astype(BF16), wm_ref[...])
    o_ref[...] = _layer_norm(DN_ALPHA * x_ref[...] + out, g_ref[...], b_ref[...])


def _merge(x2d, oa, ob, proj2d, wa, wb, wm, g, bias, tm):
    t, d = x2d.shape
    row = lambda w, cbi=0: pl.BlockSpec((tm, w), lambda i: (i, cbi))
    return pl.pallas_call(
        _merge_body,
        grid=(t // tm,),
        in_specs=[row(d), row(GDN_V), row(NSA_Q), row(d, COL_GATE_A // d), row(d, COL_GATE_B // d),
                  _resident(wa.shape), _resident(wb.shape), _resident(wm.shape), _resident(g.shape),
                  _resident(bias.shape)],
        out_specs=row(d),
        out_shape=jax.ShapeDtypeStruct((t, d), F32),
        compiler_params=_cparams("arbitrary"),
        name="merge",
    )(x2d, oa, ob, proj2d, proj2d, wa, wb, wm, g, bias)


def _memkv_body(m_ref, wk_ref, wv_ref, k_ref, v_ref):
    m = m_ref[...].astype(BF16)
    k_ref[...] = _dot(m, wk_ref[...]).astype(BF16)
    v_ref[...] = _dot(m, wv_ref[...]).astype(BF16)


def _memkv(mem2d, wk, wv, tm):
    t, d = mem2d.shape
    row = pl.BlockSpec((tm, d), lambda i: (i, 0))
    return pl.pallas_call(
        _memkv_body,
        grid=(t // tm,),
        in_specs=[row, _resident(wk.shape), _resident(wv.shape)],
        out_specs=[row, row],
        out_shape=[jax.ShapeDtypeStruct((t, d), BF16)] * 2,
        compiler_params=_cparams("arbitrary"),
        name="memkv",
    )(mem2d, wk, wv)


def _xattn_body(x_ref, k_ref, v_ref, wq_ref, wo_ref, g_ref, b_ref, o_ref):
    x = x_ref[0]
    q = _dot(x.astype(BF16), wq_ref[...]).astype(BF16)
    heads = []
    for h in range(XA_HEADS):
        sl = slice(h * XA_HEAD_DIM, (h + 1) * XA_HEAD_DIM)
        s = _dot_nt(q[:, sl], k_ref[0, :, sl]) * (XA_HEAD_DIM ** -0.5)
        m = jnp.max(s, -1, keepdims=True)
        e = jnp.exp(s - m)
        p = e * (1.0 / jnp.sum(e, -1, keepdims=True))
        heads.append(_dot(p.astype(BF16), v_ref[0, :, sl]).astype(BF16))
    o = jnp.concatenate(heads, axis=1)
    out = _dot(o, wo_ref[...])
    o_ref[0] = _layer_norm(DN_ALPHA * x + out, g_ref[...], b_ref[...])


def _xattn(x3, k3, v3, wq, wo, g, bias, tm):
    b, s, d = x3.shape
    m = k3.shape[1]
    return pl.pallas_call(
        _xattn_body,
        grid=(b, s // tm),
        in_specs=[pl.BlockSpec((1, tm, d), lambda bi, i: (bi, i, 0)),
                  pl.BlockSpec((1, m, d), lambda bi, i: (bi, 0, 0)),
                  pl.BlockSpec((1, m, d), lambda bi, i: (bi, 0, 0)),
                  _resident(wq.shape), _resident(wo.shape), _resident(g.shape), _resident(bias.shape)],
        out_specs=pl.BlockSpec((1, tm, d), lambda bi, i: (bi, i, 0)),
        out_shape=jax.ShapeDtypeStruct((b, s, d), F32),
        compiler_params=_cparams("arbitrary", "arbitrary"),
        name="xattn",
    )(x3, k3, v3, wq, wo, g, bias)


def _ffn_body(x_ref, wg_ref, wu_ref, wd_ref, g_ref, b_ref, o_ref, *, chunk):
    x = x_ref[...]
    xb = x.astype(BF16)
    out = None
    for c in range(0, wg_ref.shape[1], chunk):
        hid = _silu(_dot(xb, wg_ref[:, c:c + chunk])) * _dot(xb, wu_ref[:, c:c + chunk])
        part = _dot(hid.astype(BF16), wd_ref[c:c + chunk, :])
        out = part if out is None else out + part
    o_ref[...] = _layer_norm(DN_ALPHA * x + out, g_ref[...], b_ref[...])


def _ffn(x2d, wg, wu, wd, g, bias, tm):
    t, d = x2d.shape
    dff = wg.shape[1]
    chunk = dff // 2 if (dff // 2) % LANES == 0 else dff
    row = pl.BlockSpec((tm, d), lambda i: (i, 0))
    return pl.pallas_call(
        functools.partial(_ffn_body, chunk=chunk),
        grid=(t // tm,),
        in_specs=[row, _resident(wg.shape), _resident(wu.shape), _resident(wd.shape), _resident(g.shape),
                  _resident(bias.shape)],
        out_specs=row,
        out_shape=jax.ShapeDtypeStruct((t, d), F32),
        compiler_params=_cparams("arbitrary"),
        name="ffn",
    )(x2d, wg, wu, wd, g, bias)


def _tile(n, pref):
    t = min(n, pref)
    assert n % t == 0, (n, pref)
    return t


def _layer(x, mem, positions, w_in, conv_w, a_log, dt_bias, gdn_norm_w, cmp_pos_k, cmp_pos_v, cmp_k_w1, cmp_k_w2,
           cmp_v_w1, cmp_v_w2, w_up_gdn, w_up_nsa, w_mix_out, ln1_g, ln1_b, xa_wq, xa_wk, xa_wv, xa_wo, ln2_g,
           ln2_b, ffn_w_gate, ffn_w_up, ffn_w_down, ln3_g, ln3_b):
    b, s, d = x.shape
    t = b * s
    assert d == D_MODEL and s % WINDOW == 0 and s // SLC_BLOCK <= LANES and s >= 2 * WINDOW

    o_qkv, o_z = 0, 3 * GDN_QK
    o_a = o_z + GDN_V
    o_b = o_a + GDN_HEADS
    o_qn = o_b + GDN_HEADS
    o_kv6 = o_qn + NSA_Q
    o_gn = o_kv6 + 6 * HEAD_DIM
    o_ga = o_gn + 3 * NSA_HEADS
    o_gb = o_ga + D_MODEL
    w_main = jnp.concatenate([w_in[:, o_ga:o_gb], w_in[:, o_gb:o_gb + D_MODEL], w_in[:, o_qkv:o_z], w_in[:, o_z:o_a],
                              w_in[:, o_qn:o_kv6], w_in[:, o_kv6:o_gn]], axis=1).astype(BF16)
    n_small = 2 * GDN_HEADS + 3 * NSA_HEADS
    w_small = jnp.concatenate([w_in[:, o_a:o_qn], w_in[:, o_gn:o_ga], jnp.zeros((d, LANES - n_small), w_in.dtype)],
                              axis=1).astype(BF16)

    x2d = x.reshape(t, d)
    proj2d, small2d = _project(x2d, w_main, w_small, _tile(t, 512))
    proj3 = proj2d.reshape(b, s, N_MAIN)
    small3 = small2d.reshape(b, s, LANES)

    hp = jnp.zeros((8, LANES), F32).at[0, :GDN_HEADS].set(a_log.astype(F32)).at[1, :GDN_HEADS].set(
        dt_bias.astype(F32))
    o_gdn = _gdn(proj3, small3, conv_w.astype(F32), hp, gdn_norm_w.reshape(1, HEAD_DIM).astype(F32), _tile(s, 1024))

    half = HEAD_DIM // 2
    inv = ROPE_THETA ** (-jnp.arange(half, dtype=F32) / half)
    inv_row = jnp.concatenate([inv, inv]).reshape(1, HEAD_DIM)
    posf = positions.astype(F32)
    tq = _tile(s, 256)
    q_t, ks_aug, vs_t, kw_rope, vw_t = _nsa_prep(proj3, posf.reshape(b, s, 1), inv_row, _tile(s, 512), tq)

    n_seg = s // CMP_STRIDE
    seg_w = CMP_STRIDE * HEAD_DIM
    kc_seg = proj3[:, :, COL_KV6:COL_KV6 + HEAD_DIM].reshape(b, n_seg, seg_w)
    vc_seg = proj3[:, :, COL_KV6 + HEAD_DIM:COL_KV6 + 2 * HEAD_DIM].reshape(b, n_seg, seg_w)
    posc = posf[:, CMP_LEN - 1::CMP_STRIDE]
    posc = jnp.concatenate([posc, posc[:, -1:]], axis=1).reshape(b, n_seg, 1)
    k_cmp, v_cmp_t = _compress(kc_seg, vc_seg, cmp_pos_k.reshape(1, CMP_LEN * HEAD_DIM).astype(F32),
                             cmp_pos_v.reshape(1, CMP_LEN * HEAD_DIM).astype(F32), cmp_k_w1.astype(BF16),
                             cmp_k_w2.astype(BF16), cmp_v_w1.astype(BF16), cmp_v_w2.T.astype(BF16), posc, inv_row)
    o_nsa = _nsa_attention(q_t, small3, k_cmp, v_cmp_t, ks_aug, vs_t, kw_rope, vw_t, tq)

    vec = lambda p: p.reshape(1, d).astype(F32)
    x1 = _merge(x2d, o_gdn.reshape(t, GDN_V), o_nsa.reshape(t, NSA_Q), proj2d, w_up_gdn.astype(BF16),
                w_up_nsa.astype(BF16), w_mix_out.astype(BF16), vec(ln1_g), vec(ln1_b), _tile(t, 1024))
    mlen = mem.shape[1]
    mk, mv = _memkv(mem.reshape(b * mlen, d), xa_wk.astype(BF16), xa_wv.astype(BF16), _tile(b * mlen, 512))
    x2 = _xattn(x1.reshape(b, s, d), mk.reshape(b, mlen, d), mv.reshape(b, mlen, d), xa_wq.astype(BF16),
                xa_wo.astype(BF16), vec(ln2_g), vec(ln2_b), _tile(s, 1024))
    x3 = _ffn(x2.reshape(t, d), ffn_w_gate.astype(BF16), ffn_w_up.astype(BF16), ffn_w_down.astype(BF16),
              vec(ln3_g), vec(ln3_b), _tile(t, 512))
    return x3.reshape(b, s, d)


def kernel(x, mem, positions, w_in, gdn_conv_w, gdn_a_log, gdn_dt_bias, gdn_norm_w, cmp_pos_k, cmp_pos_v, cmp_k_w1, cmp_k_w2, cmp_v_w1, cmp_v_w2, w_up_gdn, w_up_nsa, w_mix_out, ln1_g, ln1_b, xa_wq, xa_wk, xa_wv, xa_wo, ln2_g, ln2_b, ffn_w_gate, ffn_w_up, ffn_w_down, ln3_g, ln3_b):
    for l in range(w_in.shape[0]):
        x = _layer(x, mem, positions, w_in[l], gdn_conv_w[l], gdn_a_log[l], gdn_dt_bias[l], gdn_norm_w[l],
                   cmp_pos_k[l], cmp_pos_v[l], cmp_k_w1[l], cmp_k_w2[l], cmp_v_w1[l], cmp_v_w2[l], w_up_gdn[l],
                   w_up_nsa[l], w_mix_out[l], ln1_g[l], ln1_b[l], xa_wq[l], xa_wk[l], xa_wv[l], xa_wo[l], ln2_g[l],
                   ln2_b[l], ffn_w_gate[l], ffn_w_up[l], ffn_w_down[l], ln3_g[l], ln3_b[l])
    return x
```

```python
import functools

import jax
import jax.numpy as jnp
from jax import lax
from jax.experimental import pallas as pl
from jax.experimental.pallas import tpu as pltpu

F32 = jnp.float32
BF16 = jnp.bfloat16

D_MODEL = 1024
HEAD_DIM = 128
GDN_HEADS = 4
GDN_QK = GDN_HEADS * HEAD_DIM
GDN_V = GDN_HEADS * HEAD_DIM
GDN_CONV = 4
GDN_CHUNK = 64
NSA_HEADS = 4
NSA_Q = NSA_HEADS * HEAD_DIM
CMP_LEN = 32
CMP_STRIDE = 16
SLC_BLOCK = 64
SLC_TOPK = 16
WINDOW = 512
FORCE_SCORE = 1.0e4
XA_HEADS = 4
XA_HEAD_DIM = D_MODEL // XA_HEADS
ROPE_THETA = 10000.0
LN_EPS = 1e-5
RMS_EPS = 1e-6
NEG = -1.0e30
ATTN_SCALE = HEAD_DIM ** -0.5
LOG2E = 1.4426950408889634
DEPTH = 1
DN_ALPHA = (2 * DEPTH) ** 0.25

LANES = 128
CONV_HIST = 128
VT_ONES_ROWS = 16
VMEM_LIMIT_BYTES = 56 * 1024 * 1024

COL_GATE_A = 0
COL_GATE_B = 1024
COL_GQ = 2048
COL_GK = 2560
COL_GV = 3072
COL_Z = 3584
COL_NQ = 4096
COL_KV6 = 4608
N_MAIN = 5376
SM_A = 0
SM_B = 4
SM_G = 8


def _cparams(*sem):
    return pltpu.CompilerParams(dimension_semantics=sem, vmem_limit_bytes=VMEM_LIMIT_BYTES)


def _resident(shape):
    nd = len(shape)
    return pl.BlockSpec(shape, lambda *_: (0,) * nd, pipeline_mode=pl.Buffered(1))


def _layer_norm(v, g, b):
    mu = jnp.mean(v, -1, keepdims=True)
    c = v - mu
    var = jnp.mean(c * c, -1, keepdims=True)
    return c * lax.rsqrt(var + LN_EPS) * g + b


def _silu(v):
    half = 0.5 * v
    return half + half * jnp.tanh(half)


def _sigmoid(v):
    return 0.5 + 0.5 * jnp.tanh(0.5 * v)


def _dot(a, b):
    return jnp.dot(a, b, preferred_element_type=F32)


def _dot_nt(a, b):
    return lax.dot_general(a, b, (((1,), (1,)), ((), ())), preferred_element_type=F32)


def _proj_body(x_ref, w_ref, ws_ref, o_ref, os_ref, *, chunk):
    x = x_ref[...].astype(BF16)
    for c in range(0, w_ref.shape[1], chunk):
        o_ref[:, c:c + chunk] = _dot(x, w_ref[:, c:c + chunk]).astype(o_ref.dtype)
    os_ref[...] = _dot(x, ws_ref[...])


def _project(x2d, w_main, w_small, tm):
    t, d = x2d.shape
    n = w_main.shape[1]
    return pl.pallas_call(
        functools.partial(_proj_body, chunk=768),
        grid=(t // tm,),
        in_specs=[pl.BlockSpec((tm, d), lambda i: (i, 0)), _resident((d, n)), _resident((d, LANES))],
        out_specs=[pl.BlockSpec((tm, n), lambda i: (i, 0)), pl.BlockSpec((tm, LANES), lambda i: (i, 0))],
        out_shape=[jax.ShapeDtypeStruct((t, n), BF16), jax.ShapeDtypeStruct((t, LANES), F32)],
        compiler_params=_cparams("arbitrary"),
        name="proj",
    )(x2d, w_main, w_small)


def _rope_tables(pos_col, inv_row):
    n = pos_col.shape[0]
    half_n, half_d = n // 2, HEAD_DIM // 2
    low = lax.broadcasted_iota(jnp.int32, (half_n, HEAD_DIM), 1) < half_d
    ang = jnp.where(low, pos_col[:half_n], pos_col[half_n:]) * inv_row
    cos = jnp.cos(ang)
    sin = jnp.sin(ang)
    cos_sw = pltpu.roll(cos, half_d, 1)
    sin_sw = pltpu.roll(sin, half_d, 1)
    cos_full = jnp.concatenate([jnp.where(low, cos, cos_sw), jnp.where(low, cos_sw, cos)], axis=0)
    sin_signed = jnp.concatenate([jnp.where(low, -sin, sin_sw), jnp.where(low, -sin_sw, sin)], axis=0)
    return cos_full, sin_signed


def _rope(v, cos, sin_signed):
    return v * cos + pltpu.roll(v, HEAD_DIM // 2, 1) * sin_signed


def _nsa_prep_body(pos_ref, inv_ref, q_ref, ks_ref, vs_ref, kw_ref, vw_ref,
                   qt_ref, kso_ref, vst_ref, kwo_ref, vwt_ref, *, ct, tk):
    i = pl.program_id(1)
    cos, sin_s = _rope_tables(pos_ref[0], inv_ref[...])
    for h in range(NSA_HEADS):
        sl = slice(h * HEAD_DIM, (h + 1) * HEAD_DIM)
        q_rot = _rope(q_ref[0, :, sl].astype(F32), cos, sin_s) * (ATTN_SCALE * LOG2E)
        qt_ref[0, sl, :] = q_rot.T.astype(BF16)
    kso_ref[0, :, :HEAD_DIM] = _rope(ks_ref[0].astype(F32), cos, sin_s).astype(BF16)
    lane = lax.broadcasted_iota(jnp.int32, (ct, LANES), 1)
    tok = lax.broadcasted_iota(jnp.int32, (ct, LANES), 0) + i * ct
    kso_ref[0, :, HEAD_DIM:] = jnp.where(lane == tok // SLC_BLOCK, 1.0, 0.0).astype(BF16)
    kwo_ref[0] = _rope(kw_ref[0].astype(F32), cos, sin_s).astype(BF16)
    ones = jnp.ones((VT_ONES_ROWS, tk), BF16)
    for v_ref, vt_ref in ((vs_ref, vst_ref), (vw_ref, vwt_ref)):
        v_t = v_ref[0].astype(F32).T
        for n in range(ct // tk):
            vt_ref[0, n, :HEAD_DIM, :] = v_t[:, n * tk:(n + 1) * tk].astype(BF16)
            vt_ref[0, n, HEAD_DIM:, :] = ones


def _nsa_prep(proj3, posf, inv_row, ct, tk):
    b, s, _ = proj3.shape
    kv_col = COL_KV6 // HEAD_DIM
    col = lambda c: pl.BlockSpec((1, ct, HEAD_DIM), lambda bi, i: (bi, i, c))
    vrows = HEAD_DIM + VT_ONES_ROWS
    vt_spec = pl.BlockSpec((1, ct // tk, vrows, tk), lambda bi, i: (bi, i, 0, 0))
    vt_shape = jax.ShapeDtypeStruct((b, s // tk, vrows, tk), BF16)
    return pl.pallas_call(
        functools.partial(_nsa_prep_body, ct=ct, tk=tk),
        grid=(b, s // ct),
        in_specs=[
            pl.BlockSpec((1, ct, posf.shape[2]), lambda bi, i: (bi, i, 0)),
            _resident((1, LANES)),
            pl.BlockSpec((1, ct, NSA_Q), lambda bi, i: (bi, i, COL_NQ // NSA_Q)),
            col(kv_col + 2), col(kv_col + 3), col(kv_col + 4), col(kv_col + 5),
        ],
        out_specs=[
            pl.BlockSpec((1, NSA_Q, ct), lambda bi, i: (bi, 0, i)),
            pl.BlockSpec((1, ct, 2 * HEAD_DIM), lambda bi, i: (bi, i, 0)),
            vt_spec,
            pl.BlockSpec((1, ct, HEAD_DIM), lambda bi, i: (bi, i, 0)),
            vt_spec,
        ],
        out_shape=[
            jax.ShapeDtypeStruct((b, NSA_Q, s), BF16),
            jax.ShapeDtypeStruct((b, s, 2 * HEAD_DIM), BF16),
            vt_shape,
            jax.ShapeDtypeStruct((b, s, HEAD_DIM), BF16),
            vt_shape,
        ],
        compiler_params=_cparams("arbitrary", "arbitrary"),
        name="nsa_prep",
    )(posf, inv_row, proj3, proj3, proj3, proj3, proj3)


def _compress_body(kc_ref, vc_ref, pk_ref, pv_ref, kw1_ref, kw2_ref, vw1_ref, vw2_ref, posc_ref, inv_ref,
                   ko_ref, vo_ref):
    def comp(seg_ref, p_ref, w1_ref, w2_ref, transposed):
        seg = seg_ref[0]
        n_seg, half = seg.shape
        first = _dot(seg, w1_ref[:half, :])
        second = _dot(seg, w1_ref[half:, :])
        pos_term = _dot(jnp.broadcast_to(p_ref[...], (8, p_ref.shape[1])).astype(BF16), w1_ref[...])[0:1]
        hidden = first + pltpu.roll(second, n_seg - 1, 0) + pos_term
        act = _silu(hidden)
        if transposed:
            return _dot(w2_ref[...], act.T.astype(BF16))
        return _dot(act.astype(BF16), w2_ref[...])

    kc = comp(kc_ref, pk_ref, kw1_ref, kw2_ref, False)
    cos, sin_s = _rope_tables(posc_ref[0], inv_ref[...])
    ko_ref[0] = _rope(kc, cos, sin_s).astype(BF16)
    vo_ref[0] = comp(vc_ref, pv_ref, vw1_ref, vw2_ref, True).astype(BF16)


def _compress(kc_seg, vc_seg, pos_k, pos_v, kw1, kw2, vw1, vw2, posc, inv_row):
    b, n_seg, seg_w = kc_seg.shape
    seg_spec = pl.BlockSpec((1, n_seg, seg_w), lambda bi: (bi, 0, 0))
    return pl.pallas_call(
        _compress_body,
        grid=(b,),
        in_specs=[seg_spec, seg_spec, _resident(pos_k.shape), _resident(pos_v.shape), _resident(kw1.shape),
                  _resident(kw2.shape), _resident(vw1.shape), _resident(vw2.shape),
                  pl.BlockSpec((1, n_seg, 1), lambda bi: (bi, 0, 0)), _resident((1, LANES))],
        out_specs=[pl.BlockSpec((1, n_seg, HEAD_DIM), lambda bi: (bi, 0, 0)),
                   pl.BlockSpec((1, HEAD_DIM, n_seg), lambda bi: (bi, 0, 0))],
        out_shape=[jax.ShapeDtypeStruct((b, n_seg, HEAD_DIM), BF16),
                   jax.ShapeDtypeStruct((b, HEAD_DIM, n_seg), BF16)],
        compiler_params=_cparams("arbitrary"),
        name="compress",
    )(kc_seg, vc_seg, pos_k, pos_v, kw1, kw2, vw1, vw2, posc, inv_row)


def _split3(v):
    hi = v.astype(BF16)
    r1 = v - hi.astype(F32)
    mid = r1.astype(BF16)
    lo = (r1 - mid.astype(F32)).astype(BF16)
    return hi, mid, lo


def _nsa_attn_body(qt_ref, sm_ref, kc_ref, vct_ref, ks_ref, vst_ref, kw_ref, vwt_ref, o_ref,
                   m_sc, acc_sc, m2_sc, acc2_sc, s_buf, oc_sc, imp_sc, *, tq, n_cmp_pad, n_slc, top_k):
    qi = pl.program_id(1)
    t0 = qi * tq
    hq = NSA_HEADS * tq
    q_t = jnp.concatenate([qt_ref[0, h * HEAD_DIM:(h + 1) * HEAD_DIM, :] for h in range(NSA_HEADS)], axis=1)

    def lane_query(shape):
        return lax.broadcasted_iota(jnp.int32, shape, 1) & (tq - 1)

    def compressed_branch(rows):
        s_c = _dot(kc_ref[0, :rows, :], q_t)
        n_id = lax.broadcasted_iota(jnp.int32, (rows, hq), 0)
        valid_c = (n_id * CMP_STRIDE + (CMP_LEN - 1)) <= t0 + lane_query((rows, hq))
        m_c = jnp.max(jnp.where(valid_c, s_c, NEG), 0, keepdims=True)
        e_c = jnp.where(valid_c, jnp.exp2(s_c - m_c), 0.0)
        l_c = jnp.sum(e_c, 0, keepdims=True)
        p_c = e_c * (1.0 / jnp.where(l_c > 0.0, l_c, 1.0))
        oc_sc[...] = _dot(vct_ref[0, :, :rows], p_c.astype(BF16))
        p_sum = p_c[:, 0:tq]
        for h in range(1, NSA_HEADS):
            p_sum = p_sum + p_c[:, h * tq:(h + 1) * tq]
        oj = lax.broadcasted_iota(jnp.int32, (LANES, rows), 0) * SLC_BLOCK
        on = lax.broadcasted_iota(jnp.int32, (LANES, rows), 1) * CMP_STRIDE
        overlap_t = jnp.where((on < oj + SLC_BLOCK) & (on + CMP_LEN > oj), 1.0, 0.0).astype(BF16)
        imp_sc[...] = sum(_dot(overlap_t, part) for part in _split3(p_sum))

    n_groups = n_cmp_pad // LANES
    tokens_per_group = CMP_STRIDE * LANES
    need = ((qi + 1) * tq + tokens_per_group - 1) // tokens_per_group
    for g in range(1, n_groups + 1):
        pl.when((need == g) if g < n_groups else (need >= g))(functools.partial(compressed_branch, g * LANES))
    o_c = oc_sc[...]
    imp = imp_sc[...]

    j = lax.broadcasted_iota(jnp.int32, (LANES, tq), 0)
    cur = (t0 + lax.broadcasted_iota(jnp.int32, (LANES, tq), 1)) // SLC_BLOCK
    forced = (j == 0) | (j == cur) | (j == cur - 1)
    cand = jnp.where(jnp.logical_not(forced) & (j <= cur) & (j < n_slc), imp, -1.0)
    sel = jnp.where(forced, 1.0, 0.0)
    for _ in range(max(top_k - 3, 0)):
        best = jnp.max(cand, 0, keepdims=True)
        idx = jnp.min(jnp.where(cand == best, j, LANES), 0, keepdims=True)
        hit = j == idx
        sel = jnp.maximum(sel, jnp.where(hit, jnp.where(best >= 0.0, 1.0, 0.0), 0.0))
        cand = jnp.where(hit, -2.0, cand)
    bias = jnp.where(sel > 0.0, 0.0, NEG).astype(BF16)
    q_aug = jnp.concatenate([q_t, jnp.concatenate([bias] * NSA_HEADS, axis=1)], axis=0)

    def tile_scores(k_ref, n, q_op, mask):
        k_tile = k_ref[0, pl.ds(pl.multiple_of(n * tq, tq), tq), :]
        s = _dot(k_tile, q_op)
        return (s if mask is None else jnp.where(mask, s, NEG)).astype(BF16)

    def flash_update(m_ref, acc_ref, tiles):
        m_old = m_ref[...]
        m_loc = functools.reduce(jnp.maximum, [jnp.max(s, 0, keepdims=True) for s, _ in tiles])
        m_new = jnp.maximum(m_old, m_loc.astype(F32))
        m_b = m_new.astype(BF16)
        p = jnp.concatenate([jnp.exp2(s - m_b) for s, _ in tiles], axis=0)
        vt = jnp.concatenate([vt_tile for _, vt_tile in tiles], axis=1)
        acc_ref[...] = jnp.exp2(m_old - m_new) * acc_ref[...] + _dot(vt, p)
        m_ref[...] = m_new

    def flash_init(m_ref, acc_ref):
        m_ref[...] = jnp.full(m_ref.shape, NEG, F32)
        acc_ref[...] = jnp.zeros(acc_ref.shape, F32)

    kl = lax.broadcasted_iota(jnp.int32, (tq, hq), 0)
    ql = lane_query((tq, hq))
    causal = kl <= ql

    flash_init(m_sc, acc_sc)
    n_pairs = qi // 2

    def stage_scores(slot, pair):
        keys = ks_ref[0, pl.ds(pl.multiple_of(pair * (2 * tq), 2 * tq), 2 * tq), :]
        s_buf[slot] = _dot(keys, q_aug).astype(BF16)

    def consume(slot, pair):
        vt = jnp.concatenate([vst_ref[0, 2 * pair], vst_ref[0, 2 * pair + 1]], axis=1)
        flash_update(m_sc, acc_sc, [(s_buf[slot], vt)])

    @pl.when(n_pairs > 0)
    def _():
        stage_scores(0, 0)

    def two_pairs(j, carry):
        stage_scores(1, 2 * j + 1)
        consume(0, 2 * j)
        stage_scores(0, jnp.minimum(2 * j + 2, n_pairs - 1))
        consume(1, 2 * j + 1)
        return carry

    lax.fori_loop(0, n_pairs // 2, two_pairs, 0)

    @pl.when(n_pairs % 2 == 1)
    def _():
        consume(0, n_pairs - 1)

    odd = (qi % 2) == 1
    prev = jnp.maximum(qi - 1, 0)
    flash_update(m_sc, acc_sc, [
        (tile_scores(ks_ref, prev, q_aug, jnp.broadcast_to(odd, (tq, hq))), vst_ref[0, prev]),
        (tile_scores(ks_ref, qi, q_aug, causal), vst_ref[0, qi])])

    flash_init(m2_sc, acc2_sc)
    window_tiles = []
    for back in range(0, WINDOW // tq + 1):
        n = jnp.maximum(qi - back, 0)
        if back == 0:
            mask = causal
        elif back * tq == WINDOW:
            mask = (kl > ql) & (qi >= back)
        else:
            mask = jnp.broadcast_to(qi >= back, (tq, hq))
        window_tiles.append((tile_scores(kw_ref, n, q_t, mask), vwt_ref[0, n]))
    flash_update(m2_sc, acc2_sc, window_tiles)

    acc_s, acc_w = acc_sc[...], acc2_sc[...]
    o_s = acc_s[:HEAD_DIM] * (1.0 / acc_s[HEAD_DIM:HEAD_DIM + 1])
    o_w = acc_w[:HEAD_DIM] * (1.0 / acc_w[HEAD_DIM:HEAD_DIM + 1])
    gates_t = _sigmoid(sm_ref[0]).T
    eye = jnp.where(lax.broadcasted_iota(jnp.int32, (tq, tq), 0) == lax.broadcasted_iota(jnp.int32, (tq, tq), 1),
                    1.0, 0.0).astype(BF16)
    for h in range(NSA_HEADS):
        cols = slice(h * tq, (h + 1) * tq)
        c0 = SM_G + 3 * h
        o_t = (gates_t[c0:c0 + 1, :] * o_c[:, cols] + gates_t[c0 + 1:c0 + 2, :] * o_s[:, cols]
               + gates_t[c0 + 2:c0 + 3, :] * o_w[:, cols])
        o_ref[0, :, h * HEAD_DIM:(h + 1) * HEAD_DIM] = _dot_nt(eye, o_t.astype(BF16)).astype(o_ref.dtype)


def _nsa_attention(q_t, small3, k_cmp, v_cmp_t, ks_aug, vs_t, kw_rope, vw_t, tq):
    b, _, s = q_t.shape
    n_cmp_pad = k_cmp.shape[1]
    n_slc = s // SLC_BLOCK
    top_k = min(SLC_TOPK, n_slc)
    vrows = vs_t.shape[2]
    assert WINDOW % tq == 0 and vs_t.shape[-1] == tq
    batch_spec = lambda shape: pl.BlockSpec((1,) + shape, lambda bi, i: (bi,) + (0,) * len(shape))
    hq = NSA_HEADS * tq
    flash_scratch = [pltpu.VMEM((1, hq), F32), pltpu.VMEM((vrows, hq), F32)]
    return pl.pallas_call(
        functools.partial(_nsa_attn_body, tq=tq, n_cmp_pad=n_cmp_pad, n_slc=n_slc, top_k=top_k),
        grid=(b, s // tq),
        in_specs=[
            pl.BlockSpec((1, NSA_Q, tq), lambda bi, i: (bi, 0, i)),
            pl.BlockSpec((1, tq, LANES), lambda bi, i: (bi, i, 0)),
            batch_spec((n_cmp_pad, HEAD_DIM)),
            batch_spec((HEAD_DIM, n_cmp_pad)),
            batch_spec((s, 2 * HEAD_DIM)),
            batch_spec((s // tq, vrows, tq)),
            batch_spec((s, HEAD_DIM)),
            batch_spec((s // tq, vrows, tq)),
        ],
        out_specs=pl.BlockSpec((1, tq, NSA_Q), lambda bi, i: (bi, i, 0)),
        out_shape=jax.ShapeDtypeStruct((b, s, NSA_Q), BF16),
        scratch_shapes=flash_scratch + flash_scratch + [
            pltpu.VMEM((2, 2 * tq, hq), BF16),
            pltpu.VMEM((HEAD_DIM, hq), F32),
            pltpu.VMEM((LANES, tq), F32),
        ],
        compiler_params=_cparams("arbitrary", "arbitrary"),
        name="nsa_attn",
    )(q_t, small3, k_cmp, v_cmp_t, ks_aug, vs_t, kw_rope, vw_t)


def _gdn_body(xq_ref, xk_ref, xv_ref, z_ref, sm_ref, cw_ref, hp_ref, nw_ref, o_ref,
              xpad_sc, conv_sc, state_sc, qp_sc, op_sc, mp_sc, nn_sc, eg_sc, *, ct):
    i = pl.program_id(1)
    c = GDN_CHUNK
    nc = ct // c

    @pl.when(i == 0)
    def _():
        xpad_sc[:, 0:CONV_HIST, :] = jnp.zeros((3, CONV_HIST, GDN_QK), BF16)
        state_sc[...] = jnp.zeros_like(state_sc)

    row = lax.broadcasted_iota(jnp.int32, (ct, LANES), 0)
    row_in_chunk = row & (c - 1)
    ii = lax.broadcasted_iota(jnp.int32, (nc, c, c), 1)
    jj = lax.broadcasted_iota(jnp.int32, (nc, c, c), 2)

    blk = CONV_HIST
    for part, x_ref in enumerate((xq_ref, xk_ref, xv_ref)):
        wcols = slice(part * GDN_QK, (part + 1) * GDN_QK)
        xpad_sc[part, blk:, :] = x_ref[0]
        for r in range(ct // blk):
            window = xpad_sc[part, r * blk:(r + 2) * blk, :].astype(F32)
            acc = window[blk:] * cw_ref[GDN_CONV - 1:GDN_CONV, wcols]
            for k in range(1, GDN_CONV):
                acc = acc + window[blk - k:2 * blk - k] * cw_ref[GDN_CONV - 1 - k:GDN_CONV - k, wcols]
            conv_sc[part, r * blk:(r + 1) * blk, :] = _silu(acc)
        xpad_sc[part, 0:blk, :] = xpad_sc[part, ct:ct + blk, :]

    def conv_silu(part, h):
        return conv_sc[part, :, h * HEAD_DIM:(h + 1) * HEAD_DIM]

    def l2n(v):
        return v * lax.rsqrt(jnp.sum(v * v, -1, keepdims=True) + RMS_EPS)

    sm = sm_ref[0]
    sp_in = sm + hp_ref[1:2, :]
    softplus = jnp.maximum(sp_in, 0.0) + jnp.log(1.0 + jnp.exp(-jnp.abs(sp_in)))
    g_all = -jnp.exp(hp_ref[0:1, :]) * softplus
    beta_all = _sigmoid(sm)
    for h in range(GDN_HEADS):
        q = l2n(conv_silu(0, h)) * (HEAD_DIM ** -0.5)
        k = l2n(conv_silu(1, h))
        v = conv_silu(2, h)
        g = g_all[:, SM_A + h:SM_A + h + 1]
        beta = beta_all[:, SM_B + h:SM_B + h + 1]
        gc = jnp.broadcast_to(g, (ct, LANES))
        step = 1
        while step < c:
            gc = gc + jnp.where(row_in_chunk >= step, pltpu.roll(gc, step, 0), 0.0)
            step *= 2
        egc = jnp.exp(gc)
        kb = k * beta
        vb = v * beta
        gc3 = gc.reshape(nc, c, LANES)
        g_col = gc3[:, :, :c]
        g_row = jnp.swapaxes(gc3, 1, 2)[:, :c, :]
        decay = jnp.where(ii >= jj, jnp.exp(jnp.minimum(g_col - g_row, 0.0)), 0.0)
        k3 = k.reshape(nc, c, LANES).astype(BF16)
        kk = jnp.einsum("nid,njd->nij", kb.reshape(nc, c, LANES).astype(BF16), k3, preferred_element_type=F32)
        a_mat = jnp.where(ii > jj, kk * decay, 0.0)
        n_mat = -a_mat
        pw = a_mat
        sq = 2
        while sq < c:
            pwb = pw.astype(BF16)
            pw = jnp.einsum("nij,njk->nik", pwb, pwb, preferred_element_type=F32)
            n_mat = n_mat + pw + jnp.einsum("nij,njk->nik", n_mat.astype(BF16), pw.astype(BF16),
                                            preferred_element_type=F32)
            sq *= 2
        nb = n_mat.astype(BF16)
        vb3 = vb.reshape(nc, c, LANES)
        kbg3 = (kb * egc).reshape(nc, c, LANES)
        u = vb3 + jnp.einsum("nij,njd->nid", nb, vb3.astype(BF16), preferred_element_type=F32)
        w = kbg3 + jnp.einsum("nij,njd->nid", nb, kbg3.astype(BF16), preferred_element_type=F32)
        attn = jnp.einsum("nid,njd->nij", q.reshape(nc, c, LANES).astype(BF16), k3,
                          preferred_element_type=F32) * decay
        g_last = gc3[:, c - 1:c, :]
        wb = w.astype(BF16)
        ub = u.astype(BF16)
        ab = attn.astype(BF16)
        kt_t = jnp.swapaxes(k.reshape(nc, c, LANES) * jnp.exp(g_last - gc3), 1, 2).astype(BF16)
        qd3 = (q * egc).reshape(nc, c, LANES)
        qp = qd3 - jnp.einsum("nij,njd->nid", ab, wb, preferred_element_type=F32)
        op = jnp.einsum("nij,nje->nie", ab, ub, preferred_element_type=F32)
        qp_sc[h] = qp.reshape(ct, LANES).astype(BF16)
        op_sc[h] = op.reshape(ct, LANES)
        mp_sc[h] = jnp.einsum("ndi,nie->nde", kt_t, wb, preferred_element_type=F32).astype(BF16)
        nn_sc[h] = jnp.einsum("ndi,nie->nde", kt_t, ub, preferred_element_type=F32)
        eg_sc[h] = jnp.broadcast_to(jnp.exp(g_last), (nc, 8, LANES))

    nw = nw_ref[...]

    def chunk_step(n, carry):
        rows = pl.ds(pl.multiple_of(n * c, c), c)
        for h in range(GDN_HEADS):
            state = state_sc[h]
            sb = state.astype(BF16)
            op_sc[h, rows, :] = _dot(qp_sc[h, rows, :], sb) + op_sc[h, rows, :]
            chunk_decay = eg_sc[h, n][0:1]
            state_sc[h] = state * chunk_decay - _dot(mp_sc[h, n], sb) + nn_sc[h, n]
        return carry

    lax.fori_loop(0, nc, chunk_step, 0)

    gate_rows = min(ct, 256)
    for h in range(GDN_HEADS):
        cols = slice(h * HEAD_DIM, (h + 1) * HEAD_DIM)
        for r in range(0, ct, gate_rows):
            o = op_sc[h, r:r + gate_rows, :]
            zf = z_ref[0, r:r + gate_rows, cols].astype(F32)
            o = o * lax.rsqrt(jnp.mean(o * o, -1, keepdims=True) + RMS_EPS) * nw * _silu(zf)
            o_ref[0, r:r + gate_rows, cols] = o.astype(o_ref.dtype)


def _gdn(proj3, small3, conv_w, hp, norm_w, ct):
    b, s, _ = proj3.shape
    cb = lambda col: col // GDN_QK
    blk = lambda col: pl.BlockSpec((1, ct, GDN_QK), lambda bi, i: (bi, i, cb(col)))
    hsh = (GDN_HEADS, ct, LANES)
    nc = ct // GDN_CHUNK
    return pl.pallas_call(
        functools.partial(_gdn_body, ct=ct),
        grid=(b, s // ct),
        in_specs=[blk(COL_GQ), blk(COL_GK), blk(COL_GV), blk(COL_Z),
                  pl.BlockSpec((1, ct, LANES), lambda bi, i: (bi, i, 0)),
                  _resident(conv_w.shape), _resident(hp.shape), _resident(norm_w.shape)],
        out_specs=pl.BlockSpec((1, ct, GDN_V), lambda bi, i: (bi, i, 0)),
        out_shape=jax.ShapeDtypeStruct((b, s, GDN_V), BF16),
        scratch_shapes=[
            pltpu.VMEM((3, ct + CONV_HIST, GDN_QK), BF16),
            pltpu.VMEM((3, ct, GDN_QK), F32),
            pltpu.VMEM((GDN_HEADS, HEAD_DIM, HEAD_DIM), F32),
            pltpu.VMEM(hsh, BF16),
            pltpu.VMEM(hsh, F32),
            pltpu.VMEM((GDN_HEADS, nc, HEAD_DIM, HEAD_DIM), BF16),
            pltpu.VMEM((GDN_HEADS, nc, HEAD_DIM, HEAD_DIM), F32),
            pltpu.VMEM((GDN_HEADS, nc, 8, LANES), F32),
        ],
        compiler_params=_cparams("arbitrary", "arbitrary"),
        name="gdn",
    )(proj3, proj3, proj3, proj3, small3, conv_w, hp, norm_w)


def _merge_body(x_ref, oa_ref, ob_ref, ga_ref, gb_ref, wa_ref, wb_ref, wm_ref, g_ref, b_ref, o_ref):
    ya = _dot(oa_ref[...], wa_ref[...])
    yb = _dot(ob_ref[...], wb_ref[...])
    y = _sigmoid(ga_ref[...].astype(F32)) * ya + _sigmoid(gb_ref[...].astype(F32)) * yb
    out = _dot(y.astype(BF16), wm_ref[...])
    o_ref[...] = _layer_norm(DN_ALPHA * x_ref[...] + out, g_ref[...], b_ref[...])


def _merge(x2d, oa, ob, proj2d, wa, wb, wm, g, bias, tm):
    t, d = x2d.shape
    row = lambda w, cbi=0: pl.BlockSpec((tm, w), lambda i: (i, cbi))
    return pl.pallas_call(
        _merge_body,
        grid=(t // tm,),
        in_specs=[row(d), row(GDN_V), row(NSA_Q), row(d, COL_GATE_A // d), row(d, COL_GATE_B // d),
                  _resident(wa.shape), _resident(wb.shape), _resident(wm.shape), _resident(g.shape),
                  _resident(bias.shape)],
        out_specs=row(d),
        out_shape=jax.ShapeDtypeStruct((t, d), F32),
        compiler_params=_cparams("arbitrary"),
        name="merge",
    )(x2d, oa, ob, proj2d, proj2d, wa, wb, wm, g, bias)


def _memkv_body(m_ref, wk_ref, wv_ref, k_ref, v_ref):
    m = m_ref[...].astype(BF16)
    k_ref[...] = _dot(m, wk_ref[...]).astype(BF16)
    v_ref[...] = _dot(m, wv_ref[...]).astype(BF16)


def _memkv(mem2d, wk, wv, tm):
    t, d = mem2d.shape
    row = pl.BlockSpec((tm, d), lambda i: (i, 0))
    return pl.pallas_call(
        _memkv_body,
        grid=(t // tm,),
        in_specs=[row, _resident(wk.shape), _resident(wv.shape)],
        out_specs=[row, row],
        out_shape=[jax.ShapeDtypeStruct((t, d), BF16)] * 2,
        compiler_params=_cparams("arbitrary"),
        name="memkv",
    )(mem2d, wk, wv)


def _xattn_body(x_ref, k_ref, v_ref, wq_ref, wo_ref, g_ref, b_ref, o_ref):
    x = x_ref[0]
    q = _dot(x.astype(BF16), wq_ref[...]).astype(BF16)
    heads = []
    for h in range(XA_HEADS):
        sl = slice(h * XA_HEAD_DIM, (h + 1) * XA_HEAD_DIM)
        s = _dot_nt(q[:, sl], k_ref[0, :, sl]) * (XA_HEAD_DIM ** -0.5)
        m = jnp.max(s, -1, keepdims=True)
        e = jnp.exp(s - m)
        p = e * (1.0 / jnp.sum(e, -1, keepdims=True))
        heads.append(_dot(p.astype(BF16), v_ref[0, :, sl]).astype(BF16))
    o = jnp.concatenate(heads, axis=1)
    out = _dot(o, wo_ref[...])
    o_ref[0] = _layer_norm(DN_ALPHA * x + out, g_ref[...], b_ref[...])


def _xattn(x3, k3, v3, wq, wo, g, bias, tm):
    b, s, d = x3.shape
    m = k3.shape[1]
    return pl.pallas_call(
        _xattn_body,
        grid=(b, s // tm),
        in_specs=[pl.BlockSpec((1, tm, d), lambda bi, i: (bi, i, 0)),
                  pl.BlockSpec((1, m, d), lambda bi, i: (bi, 0, 0)),
                  pl.BlockSpec((1, m, d), lambda bi, i: (bi, 0, 0)),
                  _resident(wq.shape), _resident(wo.shape), _resident(g.shape), _resident(bias.shape)],
        out_specs=pl.BlockSpec((1, tm, d), lambda bi, i: (bi, i, 0)),
        out_shape=jax.ShapeDtypeStruct((b, s, d), F32),
        compiler_params=_cparams("arbitrary", "arbitrary"),
        name="xattn",
    )(x3, k3, v3, wq, wo, g, bias)


def _ffn_body(x_ref, wg_ref, wu_ref, wd_ref, g_ref, b_ref, o_ref, *, chunk):
    x = x_ref[...]
    xb = x.astype(BF16)
    out = None
    for c in range(0, wg_ref.shape[1], chunk):
        hid = _silu(_dot(xb, wg_ref[:, c:c + chunk])) * _dot(xb, wu_ref[:, c:c + chunk])
        part = _dot(hid.astype(BF16), wd_ref[c:c + chunk, :])
        out = part if out is None else out + part
    o_ref[...] = _layer_norm(DN_ALPHA * x + out, g_ref[...], b_ref[...])


def _ffn(x2d, wg, wu, wd, g, bias, tm):
    t, d = x2d.shape
    dff = wg.shape[1]
    chunk = dff // 2 if (dff // 2) % LANES == 0 else dff
    row = pl.BlockSpec((tm, d), lambda i: (i, 0))
    return pl.pallas_call(
        functools.partial(_ffn_body, chunk=chunk),
        grid=(t // tm,),
        in_specs=[row, _resident(wg.shape), _resident(wu.shape), _resident(wd.shape), _resident(g.shape),
                  _resident(bias.shape)],
        out_specs=row,
        out_shape=jax.ShapeDtypeStruct((t, d), F32),
        compiler_params=_cparams("arbitrary"),
        name="ffn",
    )(x2d, wg, wu, wd, g, bias)


def _tile(n, pref):
    t = min(n, pref)
    assert n % t == 0, (n, pref)
    return t


def _layer(x, mem, positions, w_in, conv_w, a_log, dt_bias, gdn_norm_w, cmp_pos_k, cmp_pos_v, cmp_k_w1, cmp_k_w2,
           cmp_v_w1, cmp_v_w2, w_up_gdn, w_up_nsa, w_mix_out, ln1_g, ln1_b, xa_wq, xa_wk, xa_wv, xa_wo, ln2_g,
           ln2_b, ffn_w_gate, ffn_w_up, ffn_w_down, ln3_g, ln3_b):
    b, s, d = x.shape
    t = b * s
    assert d == D_MODEL and s % WINDOW == 0 and s // SLC_BLOCK <= LANES and s >= 2 * WINDOW

    o_qkv, o_z = 0, 3 * GDN_QK
    o_a = o_z + GDN_V
    o_b = o_a + GDN_HEADS
    o_qn = o_b + GDN_HEADS
    o_kv6 = o_qn + NSA_Q
    o_gn = o_kv6 + 6 * HEAD_DIM
    o_ga = o_gn + 3 * NSA_HEADS
    o_gb = o_ga + D_MODEL
    w_main = jnp.concatenate([w_in[:, o_ga:o_gb], w_in[:, o_gb:o_gb + D_MODEL], w_in[:, o_qkv:o_z], w_in[:, o_z:o_a],
                              w_in[:, o_qn:o_kv6], w_in[:, o_kv6:o_gn]], axis=1).astype(BF16)
    n_small = 2 * GDN_HEADS + 3 * NSA_HEADS
    w_small = jnp.concatenate([w_in[:, o_a:o_qn], w_in[:, o_gn:o_ga], jnp.zeros((d, LANES - n_small), w_in.dtype)],
                              axis=1).astype(BF16)

    x2d = x.reshape(t, d)
    proj2d, small2d = _project(x2d, w_main, w_small, _tile(t, 512))
    proj3 = proj2d.reshape(b, s, N_MAIN)
    small3 = small2d.reshape(b, s, LANES)

    hp = jnp.zeros((8, LANES), F32).at[0, :GDN_HEADS].set(a_log.astype(F32)).at[1, :GDN_HEADS].set(
        dt_bias.astype(F32))
    o_gdn = _gdn(proj3, small3, conv_w.astype(F32), hp, gdn_norm_w.reshape(1, HEAD_DIM).astype(F32), _tile(s, 1024))

    half = HEAD_DIM // 2
    inv = ROPE_THETA ** (-jnp.arange(half, dtype=F32) / half)
    inv_row = jnp.concatenate([inv, inv]).reshape(1, HEAD_DIM)
    posf = positions.astype(F32)
    tq = _tile(s, 256)
    q_t, ks_aug, vs_t, kw_rope, vw_t = _nsa_prep(proj3, jnp.broadcast_to(posf[:, :, None], (b, s, LANES)), inv_row,
                                                 _tile(s, 1024), tq)

    n_seg = s // CMP_STRIDE
    seg_w = CMP_STRIDE * HEAD_DIM
    kc_seg = proj3[:, :, COL_KV6:COL_KV6 + HEAD_DIM].reshape(b, n_seg, seg_w)
    vc_seg = proj3[:, :, COL_KV6 + HEAD_DIM:COL_KV6 + 2 * HEAD_DIM].reshape(b, n_seg, seg_w)
    posc = posf[:, CMP_LEN - 1::CMP_STRIDE]
    posc = jnp.concatenate([posc, posc[:, -1:]], axis=1).reshape(b, n_seg, 1)
    k_cmp, v_cmp_t = _compress(kc_seg, vc_seg, cmp_pos_k.reshape(1, CMP_LEN * HEAD_DIM).astype(F32),
                             cmp_pos_v.reshape(1, CMP_LEN * HEAD_DIM).astype(F32), cmp_k_w1.astype(BF16),
                             cmp_k_w2.astype(BF16), cmp_v_w1.astype(BF16), cmp_v_w2.T.astype(BF16), posc, inv_row)
    o_nsa = _nsa_attention(q_t, small3, k_cmp, v_cmp_t, ks_aug, vs_t, kw_rope, vw_t, tq)

    vec = lambda p: p.reshape(1, d).astype(F32)
    x1 = _merge(x2d, o_gdn.reshape(t, GDN_V), o_nsa.reshape(t, NSA_Q), proj2d, w_up_gdn.astype(BF16),
                w_up_nsa.astype(BF16), w_mix_out.astype(BF16), vec(ln1_g), vec(ln1_b), _tile(t, 1024))
    mlen = mem.shape[1]
    mk, mv = _memkv(mem.reshape(b * mlen, d), xa_wk.astype(BF16), xa_wv.astype(BF16), _tile(b * mlen, 512))
    x2 = _xattn(x1.reshape(b, s, d), mk.reshape(b, mlen, d), mv.reshape(b, mlen, d), xa_wq.astype(BF16),
                xa_wo.astype(BF16), vec(ln2_g), vec(ln2_b), _tile(s, 1024))
    x3 = _ffn(x2.reshape(t, d), ffn_w_gate.astype(BF16), ffn_w_up.astype(BF16), ffn_w_down.astype(BF16),
              vec(ln3_g), vec(ln3_b), _tile(t, 512))
    return x3.reshape(b, s, d)


def kernel(x, mem, positions, w_in, gdn_conv_w, gdn_a_log, gdn_dt_bias, gdn_norm_w, cmp_pos_k, cmp_pos_v, cmp_k_w1, cmp_k_w2, cmp_v_w1, cmp_v_w2, w_up_gdn, w_up_nsa, w_mix_out, ln1_g, ln1_b, xa_wq, xa_wk, xa_wv, xa_wo, ln2_g, ln2_b, ffn_w_gate, ffn_w_up, ffn_w_down, ln3_g, ln3_b):
    for l in range(w_in.shape[0]):
        x = _layer(x, mem, positions, w_in[l], gdn_conv_w[l], gdn_a_log[l], gdn_dt_bias[l], gdn_norm_w[l],
                   cmp_pos_k[l], cmp_pos_v[l], cmp_k_w1[l], cmp_k_w2[l], cmp_v_w1[l], cmp_v_w2[l], w_up_gdn[l],
                   w_up_nsa[l], w_mix_out[l], ln1_g[l], ln1_b[l], xa_wq[l], xa_wk[l], xa_wv[l], xa_wo[l], ln2_g[l],
                   ln2_b[l], ffn_w_gate[l], ffn_w_up[l], ffn_w_down[l], ln3_g[l], ln3_b[l])
    return x
```

```python
import functools

import jax
import jax.numpy as jnp
from jax import lax
from jax.experimental import pallas as pl
from jax.experimental.pallas import tpu as pltpu

F32 = jnp.float32
BF16 = jnp.bfloat16

D_MODEL = 1024
HEAD_DIM = 128
GDN_HEADS = 4
GDN_QK = GDN_HEADS * HEAD_DIM
GDN_V = GDN_HEADS * HEAD_DIM
GDN_CONV = 4
GDN_CHUNK = 64
NSA_HEADS = 4
NSA_Q = NSA_HEADS * HEAD_DIM
CMP_LEN = 32
CMP_STRIDE = 16
SLC_BLOCK = 64
SLC_TOPK = 16
WINDOW = 512
FORCE_SCORE = 1.0e4
XA_HEADS = 4
XA_HEAD_DIM = D_MODEL // XA_HEADS
ROPE_THETA = 10000.0
LN_EPS = 1e-5
RMS_EPS = 1e-6
NEG = -1.0e30
ATTN_SCALE = HEAD_DIM ** -0.5
LOG2E = 1.4426950408889634
DEPTH = 1
DN_ALPHA = (2 * DEPTH) ** 0.25

LANES = 128
CONV_HIST = 128
VT_ONES_ROWS = 16
VMEM_LIMIT_BYTES = 56 * 1024 * 1024

COL_GATE_A = 0
COL_GATE_B = 1024
COL_GQ = 2048
COL_GK = 2560
COL_GV = 3072
COL_Z = 3584
COL_NQ = 4096
COL_KV6 = 4608
N_MAIN = 5376
SM_A = 0
SM_B = 4
SM_G = 8


def _cparams(*sem):
    return pltpu.CompilerParams(dimension_semantics=sem, vmem_limit_bytes=VMEM_LIMIT_BYTES)


def _resident(shape):
    nd = len(shape)
    return pl.BlockSpec(shape, lambda *_: (0,) * nd, pipeline_mode=pl.Buffered(1))


def _layer_norm(v, g, b):
    mu = jnp.mean(v, -1, keepdims=True)
    c = v - mu
    var = jnp.mean(c * c, -1, keepdims=True)
    return c * lax.rsqrt(var + LN_EPS) * g + b


def _silu(v):
    half = 0.5 * v
    return half + half * jnp.tanh(half)


def _sigmoid(v):
    return 0.5 + 0.5 * jnp.tanh(0.5 * v)


def _dot(a, b):
    return jnp.dot(a, b, preferred_element_type=F32)


def _dot_nt(a, b):
    return lax.dot_general(a, b, (((1,), (1,)), ((), ())), preferred_element_type=F32)


def _proj_body(x_ref, w_ref, ws_ref, o_ref, os_ref, *, chunk):
    x = x_ref[...].astype(BF16)
    for c in range(0, w_ref.shape[1], chunk):
        o_ref[:, c:c + chunk] = _dot(x, w_ref[:, c:c + chunk]).astype(o_ref.dtype)
    os_ref[...] = _dot(x, ws_ref[...])


def _project(x2d, w_main, w_small, tm):
    t, d = x2d.shape
    n = w_main.shape[1]
    return pl.pallas_call(
        functools.partial(_proj_body, chunk=768),
        grid=(t // tm,),
        in_specs=[pl.BlockSpec((tm, d), lambda i: (i, 0)), _resident((d, n)), _resident((d, LANES))],
        out_specs=[pl.BlockSpec((tm, n), lambda i: (i, 0)), pl.BlockSpec((tm, LANES), lambda i: (i, 0))],
        out_shape=[jax.ShapeDtypeStruct((t, n), BF16), jax.ShapeDtypeStruct((t, LANES), F32)],
        compiler_params=_cparams("arbitrary"),
        name="proj",
    )(x2d, w_main, w_small)


def _rope_tables(pos_col, inv_row):
    n = pos_col.shape[0]
    half_n, half_d = n // 2, HEAD_DIM // 2
    low = lax.broadcasted_iota(jnp.int32, (half_n, HEAD_DIM), 1) < half_d
    ang = jnp.where(low, pos_col[:half_n], pos_col[half_n:]) * inv_row
    cos = jnp.cos(ang)
    sin = jnp.sin(ang)
    cos_sw = pltpu.roll(cos, half_d, 1)
    sin_sw = pltpu.roll(sin, half_d, 1)
    cos_full = jnp.concatenate([jnp.where(low, cos, cos_sw), jnp.where(low, cos_sw, cos)], axis=0)
    sin_signed = jnp.concatenate([jnp.where(low, -sin, sin_sw), jnp.where(low, -sin_sw, sin)], axis=0)
    return cos_full, sin_signed


def _rope(v, cos, sin_signed):
    return v * cos + pltpu.roll(v, HEAD_DIM // 2, 1) * sin_signed


def _nsa_prep_body(pos_ref, inv_ref, q_ref, ks_ref, vs_ref, kw_ref, vw_ref,
                   qt_ref, kso_ref, vst_ref, kwo_ref, vwt_ref, *, ct, tk):
    i = pl.program_id(1)
    cos, sin_s = _rope_tables(pos_ref[0], inv_ref[...])
    for h in range(NSA_HEADS):
        sl = slice(h * HEAD_DIM, (h + 1) * HEAD_DIM)
        q_rot = _rope(q_ref[0, :, sl].astype(F32), cos, sin_s) * (ATTN_SCALE * LOG2E)
        qt_ref[0, sl, :] = q_rot.T.astype(BF16)
    kso_ref[0, :, :HEAD_DIM] = _rope(ks_ref[0].astype(F32), cos, sin_s).astype(BF16)
    lane = lax.broadcasted_iota(jnp.int32, (ct, LANES), 1)
    tok = lax.broadcasted_iota(jnp.int32, (ct, LANES), 0) + i * ct
    kso_ref[0, :, HEAD_DIM:] = jnp.where(lane == tok // SLC_BLOCK, 1.0, 0.0).astype(BF16)
    kwo_ref[0] = _rope(kw_ref[0].astype(F32), cos, sin_s).astype(BF16)
    ones = jnp.ones((VT_ONES_ROWS, tk), BF16)
    for v_ref, vt_ref in ((vs_ref, vst_ref), (vw_ref, vwt_ref)):
        v_t = v_ref[0].astype(F32).T
        for n in range(ct // tk):
            vt_ref[0, n, :HEAD_DIM, :] = v_t[:, n * tk:(n + 1) * tk].astype(BF16)
            vt_ref[0, n, HEAD_DIM:, :] = ones


def _nsa_prep(proj3, posf, inv_row, ct, tk):
    b, s, _ = proj3.shape
    kv_col = COL_KV6 // HEAD_DIM
    col = lambda c: pl.BlockSpec((1, ct, HEAD_DIM), lambda bi, i: (bi, i, c))
    vrows = HEAD_DIM + VT_ONES_ROWS
    vt_spec = pl.BlockSpec((1, ct // tk, vrows, tk), lambda bi, i: (bi, i, 0, 0))
    vt_shape = jax.ShapeDtypeStruct((b, s // tk, vrows, tk), BF16)
    return pl.pallas_call(
        functools.partial(_nsa_prep_body, ct=ct, tk=tk),
        grid=(b, s // ct),
        in_specs=[
            pl.BlockSpec((1, ct, posf.shape[2]), lambda bi, i: (bi, i, 0)),
            _resident((1, LANES)),
            pl.BlockSpec((1, ct, NSA_Q), lambda bi, i: (bi, i, COL_NQ // NSA_Q)),
            col(kv_col + 2), col(kv_col + 3), col(kv_col + 4), col(kv_col + 5),
        ],
        out_specs=[
            pl.BlockSpec((1, NSA_Q, ct), lambda bi, i: (bi, 0, i)),
            pl.BlockSpec((1, ct, 2 * HEAD_DIM), lambda bi, i: (bi, i, 0)),
            vt_spec,
            pl.BlockSpec((1, ct, HEAD_DIM), lambda bi, i: (bi, i, 0)),
            vt_spec,
        ],
        out_shape=[
            jax.ShapeDtypeStruct((b, NSA_Q, s), BF16),
            jax.ShapeDtypeStruct((b, s, 2 * HEAD_DIM), BF16),
            vt_shape,
            jax.ShapeDtypeStruct((b, s, HEAD_DIM), BF16),
            vt_shape,
        ],
        compiler_params=_cparams("arbitrary", "arbitrary"),
        name="nsa_prep",
    )(posf, inv_row, proj3, proj3, proj3, proj3, proj3)


def _compress_body(kc_ref, vc_ref, pk_ref, pv_ref, kw1_ref, kw2_ref, vw1_ref, vw2_ref, posc_ref, inv_ref,
                   ko_ref, vo_ref):
    def comp(seg_ref, p_ref, w1_ref, w2_ref, transposed):
        seg = seg_ref[0]
        n_seg, half = seg.shape
        first = _dot(seg, w1_ref[:half, :])
        second = _dot(seg, w1_ref[half:, :])
        pos_term = _dot(jnp.broadcast_to(p_ref[...], (8, p_ref.shape[1])).astype(BF16), w1_ref[...])[0:1]
        hidden = first + pltpu.roll(second, n_seg - 1, 0) + pos_term
        act = _silu(hidden)
        if transposed:
            return _dot(w2_ref[...], act.T.astype(BF16))
        return _dot(act.astype(BF16), w2_ref[...])

    kc = comp(kc_ref, pk_ref, kw1_ref, kw2_ref, False)
    cos, sin_s = _rope_tables(posc_ref[0], inv_ref[...])
    ko_ref[0] = _rope(kc, cos, sin_s).astype(BF16)
    vo_ref[0] = comp(vc_ref, pv_ref, vw1_ref, vw2_ref, True).astype(BF16)


def _compress(kc_seg, vc_seg, pos_k, pos_v, kw1, kw2, vw1, vw2, posc, inv_row):
    b, n_seg, seg_w = kc_seg.shape
    seg_spec = pl.BlockSpec((1, n_seg, seg_w), lambda bi: (bi, 0, 0))
    return pl.pallas_call(
        _compress_body,
        grid=(b,),
        in_specs=[seg_spec, seg_spec, _resident(pos_k.shape), _resident(pos_v.shape), _resident(kw1.shape),
                  _resident(kw2.shape), _resident(vw1.shape), _resident(vw2.shape),
                  pl.BlockSpec((1, n_seg, 1), lambda bi: (bi, 0, 0)), _resident((1, LANES))],
        out_specs=[pl.BlockSpec((1, n_seg, HEAD_DIM), lambda bi: (bi, 0, 0)),
                   pl.BlockSpec((1, HEAD_DIM, n_seg), lambda bi: (bi, 0, 0))],
        out_shape=[jax.ShapeDtypeStruct((b, n_seg, HEAD_DIM), BF16),
                   jax.ShapeDtypeStruct((b, HEAD_DIM, n_seg), BF16)],
        compiler_params=_cparams("arbitrary"),
        name="compress",
    )(kc_seg, vc_seg, pos_k, pos_v, kw1, kw2, vw1, vw2, posc, inv_row)


def _split3(v):
    hi = v.astype(BF16)
    r1 = v - hi.astype(F32)
    mid = r1.astype(BF16)
    lo = (r1 - mid.astype(F32)).astype(BF16)
    return hi, mid, lo


def _nsa_attn_body(qt_ref, sm_ref, kc_ref, vct_ref, ks_ref, vst_ref, kw_ref, vwt_ref, o_ref,
                   m_sc, acc_sc, m2_sc, acc2_sc, s_buf, oc_sc, imp_sc, *, tq, n_cmp_pad, n_slc, top_k):
    qi = pl.program_id(1)
    t0 = qi * tq
    hq = NSA_HEADS * tq
    q_t = jnp.concatenate([qt_ref[0, h * HEAD_DIM:(h + 1) * HEAD_DIM, :] for h in range(NSA_HEADS)], axis=1)

    def lane_query(shape):
        return lax.broadcasted_iota(jnp.int32, shape, 1) & (tq - 1)

    def compressed_branch(rows):
        s_c = _dot(kc_ref[0, :rows, :], q_t)
        n_id = lax.broadcasted_iota(jnp.int32, (rows, hq), 0)
        valid_c = (n_id * CMP_STRIDE + (CMP_LEN - 1)) <= t0 + lane_query((rows, hq))
        m_c = jnp.max(jnp.where(valid_c, s_c, NEG), 0, keepdims=True)
        e_c = jnp.where(valid_c, jnp.exp2(s_c - m_c), 0.0)
        l_c = jnp.sum(e_c, 0, keepdims=True)
        p_c = e_c * (1.0 / jnp.where(l_c > 0.0, l_c, 1.0))
        oc_sc[...] = _dot(vct_ref[0, :, :rows], p_c.astype(BF16))
        p_sum = p_c[:, 0:tq]
        for h in range(1, NSA_HEADS):
            p_sum = p_sum + p_c[:, h * tq:(h + 1) * tq]
        oj = lax.broadcasted_iota(jnp.int32, (LANES, rows), 0) * SLC_BLOCK
        on = lax.broadcasted_iota(jnp.int32, (LANES, rows), 1) * CMP_STRIDE
        overlap_t = jnp.where((on < oj + SLC_BLOCK) & (on + CMP_LEN > oj), 1.0, 0.0).astype(BF16)
        imp_sc[...] = sum(_dot(overlap_t, part) for part in _split3(p_sum))

    n_groups = n_cmp_pad // LANES
    tokens_per_group = CMP_STRIDE * LANES
    need = ((qi + 1) * tq + tokens_per_group - 1) // tokens_per_group
    for g in range(1, n_groups + 1):
        pl.when((need == g) if g < n_groups else (need >= g))(functools.partial(compressed_branch, g * LANES))
    o_c = oc_sc[...]
    imp = imp_sc[...]

    j = lax.broadcasted_iota(jnp.int32, (LANES, tq), 0)
    cur = (t0 + lax.broadcasted_iota(jnp.int32, (LANES, tq), 1)) // SLC_BLOCK
    forced = (j == 0) | (j == cur) | (j == cur - 1)
    cand = jnp.where(jnp.logical_not(forced) & (j <= cur) & (j < n_slc), imp, -1.0)
    sel = jnp.where(forced, 1.0, 0.0)
    for _ in range(max(top_k - 3, 0)):
        best = jnp.max(cand, 0, keepdims=True)
        idx = jnp.min(jnp.where(cand == best, j, LANES), 0, keepdims=True)
        hit = j == idx
        sel = jnp.maximum(sel, jnp.where(hit, jnp.where(best >= 0.0, 1.0, 0.0), 0.0))
        cand = jnp.where(hit, -2.0, cand)
    bias = jnp.where(sel > 0.0, 0.0, NEG).astype(BF16)
    q_aug = jnp.concatenate([q_t, jnp.concatenate([bias] * NSA_HEADS, axis=1)], axis=0)

    def tile_scores(k_ref, n, q_op, mask):
        k_tile = k_ref[0, pl.ds(pl.multiple_of(n * tq, tq), tq), :]
        s = _dot(k_tile, q_op)
        return (s if mask is None else jnp.where(mask, s, NEG)).astype(BF16)

    def flash_update(m_ref, acc_ref, tiles):
        m_old = m_ref[...]
        m_loc = functools.reduce(jnp.maximum, [jnp.max(s, 0, keepdims=True) for s, _ in tiles])
        m_new = jnp.maximum(m_old, m_loc.astype(F32))
        m_b = m_new.astype(BF16)
        p = jnp.concatenate([jnp.exp2(s - m_b) for s, _ in tiles], axis=0)
        vt = jnp.concatenate([vt_tile for _, vt_tile in tiles], axis=1)
        acc_ref[...] = jnp.exp2(m_old - m_new) * acc_ref[...] + _dot(vt, p)
        m_ref[...] = m_new

    def flash_init(m_ref, acc_ref):
        m_ref[...] = jnp.full(m_ref.shape, NEG, F32)
        acc_ref[...] = jnp.zeros(acc_ref.shape, F32)

    kl = lax.broadcasted_iota(jnp.int32, (tq, hq), 0)
    ql = lane_query((tq, hq))
    causal = kl <= ql

    flash_init(m_sc, acc_sc)
    n_pairs = qi // 2

    def stage_scores(slot, pair):
        keys = ks_ref[0, pl.ds(pl.multiple_of(pair * (2 * tq), 2 * tq), 2 * tq), :]
        s_buf[slot] = _dot(keys, q_aug).astype(BF16)

    def consume(slot, pair):
        vt = jnp.concatenate([vst_ref[0, 2 * pair], vst_ref[0, 2 * pair + 1]], axis=1)
        flash_update(m_sc, acc_sc, [(s_buf[slot], vt)])

    @pl.when(n_pairs > 0)
    def _():
        stage_scores(0, 0)

    def two_pairs(j, carry):
        stage_scores(1, 2 * j + 1)
        consume(0, 2 * j)
        stage_scores(0, jnp.minimum(2 * j + 2, n_pairs - 1))
        consume(1, 2 * j + 1)
        return carry

    lax.fori_loop(0, n_pairs // 2, two_pairs, 0)

    @pl.when(n_pairs % 2 == 1)
    def _():
        consume(0, n_pairs - 1)

    odd = (qi % 2) == 1
    prev = jnp.maximum(qi - 1, 0)
    flash_update(m_sc, acc_sc, [
        (tile_scores(ks_ref, prev, q_aug, jnp.broadcast_to(odd, (tq, hq))), vst_ref[0, prev]),
        (tile_scores(ks_ref, qi, q_aug, causal), vst_ref[0, qi])])

    flash_init(m2_sc, acc2_sc)
    window_tiles = []
    for back in range(0, WINDOW // tq + 1):
        n = jnp.maximum(qi - back, 0)
        if back == 0:
            mask = causal
        elif back * tq == WINDOW:
            mask = (kl > ql) & (qi >= back)
        else:
            mask = jnp.broadcast_to(qi >= back, (tq, hq))
        window_tiles.append((tile_scores(kw_ref, n, q_t, mask), vwt_ref[0, n]))
    flash_update(m2_sc, acc2_sc, window_tiles)

    acc_s, acc_w = acc_sc[...], acc2_sc[...]
    o_s = acc_s[:HEAD_DIM] * (1.0 / acc_s[HEAD_DIM:HEAD_DIM + 1])
    o_w = acc_w[:HEAD_DIM] * (1.0 / acc_w[HEAD_DIM:HEAD_DIM + 1])
    gates_t = _sigmoid(sm_ref[0]).T
    eye = jnp.where(lax.broadcasted_iota(jnp.int32, (tq, tq), 0) == lax.broadcasted_iota(jnp.int32, (tq, tq), 1),
                    1.0, 0.0).astype(BF16)
    for h in range(NSA_HEADS):
        cols = slice(h * tq, (h + 1) * tq)
        c0 = SM_G + 3 * h
        o_t = (gates_t[c0:c0 + 1, :] * o_c[:, cols] + gates_t[c0 + 1:c0 + 2, :] * o_s[:, cols]
               + gates_t[c0 + 2:c0 + 3, :] * o_w[:, cols])
        o_ref[0, :, h * HEAD_DIM:(h + 1) * HEAD_DIM] = _dot_nt(eye, o_t.astype(BF16)).astype(o_ref.dtype)


def _nsa_attention(q_t, small3, k_cmp, v_cmp_t, ks_aug, vs_t, kw_rope, vw_t, tq):
    b, _, s = q_t.shape
    n_cmp_pad = k_cmp.shape[1]
    n_slc = s // SLC_BLOCK
    top_k = min(SLC_TOPK, n_slc)
    vrows = vs_t.shape[2]
    assert WINDOW % tq == 0 and vs_t.shape[-1] == tq
    batch_spec = lambda shape: pl.BlockSpec((1,) + shape, lambda bi, i: (bi,) + (0,) * len(shape))
    hq = NSA_HEADS * tq
    flash_scratch = [pltpu.VMEM((1, hq), F32), pltpu.VMEM((vrows, hq), F32)]
    return pl.pallas_call(
        functools.partial(_nsa_attn_body, tq=tq, n_cmp_pad=n_cmp_pad, n_slc=n_slc, top_k=top_k),
        grid=(b, s // tq),
        in_specs=[
            pl.BlockSpec((1, NSA_Q, tq), lambda bi, i: (bi, 0, i)),
            pl.BlockSpec((1, tq, LANES), lambda bi, i: (bi, i, 0)),
            batch_spec((n_cmp_pad, HEAD_DIM)),
            batch_spec((HEAD_DIM, n_cmp_pad)),
            batch_spec((s, 2 * HEAD_DIM)),
            batch_spec((s // tq, vrows, tq)),
            batch_spec((s, HEAD_DIM)),
            batch_spec((s // tq, vrows, tq)),
        ],
        out_specs=pl.BlockSpec((1, tq, NSA_Q), lambda bi, i: (bi, i, 0)),
        out_shape=jax.ShapeDtypeStruct((b, s, NSA_Q), BF16),
        scratch_shapes=flash_scratch + flash_scratch + [
            pltpu.VMEM((2, 2 * tq, hq), BF16),
            pltpu.VMEM((HEAD_DIM, hq), F32),
            pltpu.VMEM((LANES, tq), F32),
        ],
        compiler_params=_cparams("arbitrary", "arbitrary"),
        name="nsa_attn",
    )(q_t, small3, k_cmp, v_cmp_t, ks_aug, vs_t, kw_rope, vw_t)


def _gdn_body(xq_ref, xk_ref, xv_ref, z_ref, sm_ref, cw_ref, hp_ref, nw_ref, o_ref,
              xpad_sc, conv_sc, state_sc, qp_sc, op_sc, mp_sc, nn_sc, eg_sc, *, ct):
    i = pl.program_id(1)
    c = GDN_CHUNK
    nc = ct // c

    @pl.when(i == 0)
    def _():
        xpad_sc[:, 0:CONV_HIST, :] = jnp.zeros((3, CONV_HIST, GDN_QK), BF16)
        state_sc[...] = jnp.zeros_like(state_sc)

    row = lax.broadcasted_iota(jnp.int32, (ct, LANES), 0)
    row_in_chunk = row & (c - 1)
    ii = lax.broadcasted_iota(jnp.int32, (nc, c, c), 1)
    jj = lax.broadcasted_iota(jnp.int32, (nc, c, c), 2)

    blk = CONV_HIST
    for part, x_ref in enumerate((xq_ref, xk_ref, xv_ref)):
        wcols = slice(part * GDN_QK, (part + 1) * GDN_QK)
        xpad_sc[part, blk:, :] = x_ref[0]
        for r in range(ct // blk):
            window = xpad_sc[part, r * blk:(r + 2) * blk, :].astype(F32)
            acc = window[blk:] * cw_ref[GDN_CONV - 1:GDN_CONV, wcols]
            for k in range(1, GDN_CONV):
                acc = acc + window[blk - k:2 * blk - k] * cw_ref[GDN_CONV - 1 - k:GDN_CONV - k, wcols]
            conv_sc[part, r * blk:(r + 1) * blk, :] = _silu(acc)
        xpad_sc[part, 0:blk, :] = xpad_sc[part, ct:ct + blk, :]

    def conv_silu(part, h):
        return conv_sc[part, :, h * HEAD_DIM:(h + 1) * HEAD_DIM]

    ones_sq = jnp.ones((HEAD_DIM, HEAD_DIM), BF16)

    def l2n(v):
        return v * lax.rsqrt(_dot((v * v).astype(BF16), ones_sq) + RMS_EPS)

    sm = sm_ref[0]
    sp_in = sm + hp_ref[1:2, :]
    softplus = jnp.maximum(sp_in, 0.0) + jnp.log(1.0 + jnp.exp(-jnp.abs(sp_in)))
    g_all = -jnp.exp(hp_ref[0:1, :]) * softplus
    beta_all = _sigmoid(sm)
    for h in range(GDN_HEADS):
        q = l2n(conv_silu(0, h)) * (HEAD_DIM ** -0.5)
        k = l2n(conv_silu(1, h))
        v = conv_silu(2, h)
        g = g_all[:, SM_A + h:SM_A + h + 1]
        beta = beta_all[:, SM_B + h:SM_B + h + 1]
        gc = jnp.broadcast_to(g, (ct, LANES))
        step = 1
        while step < c:
            gc = gc + jnp.where(row_in_chunk >= step, pltpu.roll(gc, step, 0), 0.0)
            step *= 2
        egc = jnp.exp(gc)
        kb = k * beta
        vb = v * beta
        gc3 = gc.reshape(nc, c, LANES)
        g_col = gc3[:, :, :c]
        g_row = jnp.swapaxes(gc3, 1, 2)[:, :c, :]
        decay = jnp.where(ii >= jj, jnp.exp(jnp.minimum(g_col - g_row, 0.0)), 0.0)
        k3 = k.reshape(nc, c, LANES).astype(BF16)
        kk = jnp.einsum("nid,njd->nij", kb.reshape(nc, c, LANES).astype(BF16), k3, preferred_element_type=F32)
        a_mat = jnp.where(ii > jj, kk * decay, 0.0)
        n_mat = -a_mat
        pw = a_mat
        sq = 2
        while sq < c:
            pwb = pw.astype(BF16)
            pw = jnp.einsum("nij,njk->nik", pwb, pwb, preferred_element_type=F32)
            n_mat = n_mat + pw + jnp.einsum("nij,njk->nik", n_mat.astype(BF16), pw.astype(BF16),
                                            preferred_element_type=F32)
            sq *= 2
        nb = n_mat.astype(BF16)
        vb3 = vb.reshape(nc, c, LANES)
        kbg3 = (kb * egc).reshape(nc, c, LANES)
        u = vb3 + jnp.einsum("nij,njd->nid", nb, vb3.astype(BF16), preferred_element_type=F32)
        w = kbg3 + jnp.einsum("nij,njd->nid", nb, kbg3.astype(BF16), preferred_element_type=F32)
        attn = jnp.einsum("nid,njd->nij", q.reshape(nc, c, LANES).astype(BF16), k3,
                          preferred_element_type=F32) * decay
        g_last = gc3[:, c - 1:c, :]
        wb = w.astype(BF16)
        ub = u.astype(BF16)
        ab = attn.astype(BF16)
        kt_t = jnp.swapaxes(k.reshape(nc, c, LANES) * jnp.exp(g_last - gc3), 1, 2).astype(BF16)
        qd3 = (q * egc).reshape(nc, c, LANES)
        qp = qd3 - jnp.einsum("nij,njd->nid", ab, wb, preferred_element_type=F32)
        op = jnp.einsum("nij,nje->nie", ab, ub, preferred_element_type=F32)
        qp_sc[h] = qp.reshape(ct, LANES).astype(BF16)
        op_sc[h] = op.reshape(ct, LANES)
        mp_sc[h] = jnp.einsum("ndi,nie->nde", kt_t, wb, preferred_element_type=F32).astype(BF16)
        nn_sc[h] = jnp.einsum("ndi,nie->nde", kt_t, ub, preferred_element_type=F32)
        eg_sc[h] = jnp.broadcast_to(jnp.exp(g_last), (nc, 8, LANES))

    nw = nw_ref[...]

    def chunk_step(n, carry):
        rows = pl.ds(pl.multiple_of(n * c, c), c)
        for h in range(GDN_HEADS):
            state = state_sc[h]
            sb = state.astype(BF16)
            op_sc[h, rows, :] = _dot(qp_sc[h, rows, :], sb) + op_sc[h, rows, :]
            chunk_decay = eg_sc[h, n][0:1]
            state_sc[h] = state * chunk_decay - _dot(mp_sc[h, n], sb) + nn_sc[h, n]
        return carry

    lax.fori_loop(0, nc, chunk_step, 0)

    gate_rows = min(ct, 256)
    for h in range(GDN_HEADS):
        cols = slice(h * HEAD_DIM, (h + 1) * HEAD_DIM)
        for r in range(0, ct, gate_rows):
            o = op_sc[h, r:r + gate_rows, :]
            zf = z_ref[0, r:r + gate_rows, cols].astype(F32)
            o = o * lax.rsqrt(jnp.mean(o * o, -1, keepdims=True) + RMS_EPS) * nw * _silu(zf)
            o_ref[0, r:r + gate_rows, cols] = o.astype(o_ref.dtype)


def _gdn(proj3, small3, conv_w, hp, norm_w, ct):
    b, s, _ = proj3.shape
    cb = lambda col: col // GDN_QK
    blk = lambda col: pl.BlockSpec((1, ct, GDN_QK), lambda bi, i: (bi, i, cb(col)))
    hsh = (GDN_HEADS, ct, LANES)
    nc = ct // GDN_CHUNK
    return pl.pallas_call(
        functools.partial(_gdn_body, ct=ct),
        grid=(b, s // ct),
        in_specs=[blk(COL_GQ), blk(COL_GK), blk(COL_GV), blk(COL_Z),
                  pl.BlockSpec((1, ct, LANES), lambda bi, i: (bi, i, 0)),
                  _resident(conv_w.shape), _resident(hp.shape), _resident(norm_w.shape)],
        out_specs=pl.BlockSpec((1, ct, GDN_V), lambda bi, i: (bi, i, 0)),
        out_shape=jax.ShapeDtypeStruct((b, s, GDN_V), BF16),
        scratch_shapes=[
            pltpu.VMEM((3, ct + CONV_HIST, GDN_QK), BF16),
            pltpu.VMEM((3, ct, GDN_QK), F32),
            pltpu.VMEM((GDN_HEADS, HEAD_DIM, HEAD_DIM), F32),
            pltpu.VMEM(hsh, BF16),
            pltpu.VMEM(hsh, F32),
            pltpu.VMEM((GDN_HEADS, nc, HEAD_DIM, HEAD_DIM), BF16),
            pltpu.VMEM((GDN_HEADS, nc, HEAD_DIM, HEAD_DIM), F32),
            pltpu.VMEM((GDN_HEADS, nc, 8, LANES), F32),
        ],
        compiler_params=_cparams("arbitrary", "arbitrary"),
        name="gdn",
    )(proj3, proj3, proj3, proj3, small3, conv_w, hp, norm_w)


def _merge_body(x_ref, oa_ref, ob_ref, ga_ref, gb_ref, wa_ref, wb_ref, wm_ref, g_ref, b_ref, o_ref):
    ya = _dot(oa_ref[...], wa_ref[...])
    yb = _dot(ob_ref[...], wb_ref[...])
    y = _sigmoid(ga_ref[...].astype(F32)) * ya + _sigmoid(gb_ref[...].astype(F32)) * yb
    out = _dot(y.astype(BF16), wm_ref[...])
    o_ref[...] = _layer_norm(DN_ALPHA * x_ref[...] + out, g_ref[...], b_ref[...])


def _merge(x2d, oa, ob, proj2d, wa, wb, wm, g, bias, tm):
    t, d = x2d.shape
    row = lambda w, cbi=0: pl.BlockSpec((tm, w), lambda i: (i, cbi))
    return pl.pallas_call(
        _merge_body,
        grid=(t // tm,),
        in_specs=[row(d), row(GDN_V), row(NSA_Q), row(d, COL_GATE_A // d), row(d, COL_GATE_B // d),
                  _resident(wa.shape), _resident(wb.shape), _resident(wm.shape), _resident(g.shape),
                  _resident(bias.shape)],
        out_specs=row(d),
        out_shape=jax.ShapeDtypeStruct((t, d), F32),
        compiler_params=_cparams("arbitrary"),
        name="merge",
    )(x2d, oa, ob, proj2d, proj2d, wa, wb, wm, g, bias)


def _memkv_body(m_ref, wk_ref, wv_ref, k_ref, v_ref):
    m = m_ref[...].astype(BF16)
    k_ref[...] = _dot(m, wk_ref[...]).astype(BF16)
    v_ref[...] = _dot(m, wv_ref[...]).astype(BF16)


def _memkv(mem2d, wk, wv, tm):
    t, d = mem2d.shape
    row = pl.BlockSpec((tm, d), lambda i: (i, 0))
    return pl.pallas_call(
        _memkv_body,
        grid=(t // tm,),
        in_specs=[row, _resident(wk.shape), _resident(wv.shape)],
        out_specs=[row, row],
        out_shape=[jax.ShapeDtypeStruct((t, d), BF16)] * 2,
        compiler_params=_cparams("arbitrary"),
        name="memkv",
    )(mem2d, wk, wv)


def _xattn_body(x_ref, k_ref, v_ref, wq_ref, wo_ref, g_ref, b_ref, o_ref):
    x = x_ref[0]
    q = _dot(x.astype(BF16), wq_ref[...]).astype(BF16)
    heads = []
    for h in range(XA_HEADS):
        sl = slice(h * XA_HEAD_DIM, (h + 1) * XA_HEAD_DIM)
        s = _dot_nt(q[:, sl], k_ref[0, :, sl]) * (XA_HEAD_DIM ** -0.5)
        m = jnp.max(s, -1, keepdims=True)
        e = jnp.exp(s - m)
        p = e * (1.0 / jnp.sum(e, -1, keepdims=True))
        heads.append(_dot(p.astype(BF16), v_ref[0, :, sl]).astype(BF16))
    o = jnp.concatenate(heads, axis=1)
    out = _dot(o, wo_ref[...])
    o_ref[0] = _layer_norm(DN_ALPHA * x + out, g_ref[...], b_ref[...])


def _xattn(x3, k3, v3, wq, wo, g, bias, tm):
    b, s, d = x3.shape
    m = k3.shape[1]
    return pl.pallas_call(
        _xattn_body,
        grid=(b, s // tm),
        in_specs=[pl.BlockSpec((1, tm, d), lambda bi, i: (bi, i, 0)),
                  pl.BlockSpec((1, m, d), lambda bi, i: (bi, 0, 0)),
                  pl.BlockSpec((1, m, d), lambda bi, i: (bi, 0, 0)),
                  _resident(wq.shape), _resident(wo.shape), _resident(g.shape), _resident(bias.shape)],
        out_specs=pl.BlockSpec((1, tm, d), lambda bi, i: (bi, i, 0)),
        out_shape=jax.ShapeDtypeStruct((b, s, d), F32),
        compiler_params=_cparams("arbitrary", "arbitrary"),
        name="xattn",
    )(x3, k3, v3, wq, wo, g, bias)


def _ffn_body(x_ref, wg_ref, wu_ref, wd_ref, g_ref, b_ref, o_ref, *, chunk):
    x = x_ref[...]
    xb = x.astype(BF16)
    out = None
    for c in range(0, wg_ref.shape[1], chunk):
        hid = _silu(_dot(xb, wg_ref[:, c:c + chunk])) * _dot(xb, wu_ref[:, c:c + chunk])
        part = _dot(hid.astype(BF16), wd_ref[c:c + chunk, :])
        out = part if out is None else out + part
    o_ref[...] = _layer_norm(DN_ALPHA * x + out, g_ref[...], b_ref[...])


def _ffn(x2d, wg, wu, wd, g, bias, tm):
    t, d = x2d.shape
    dff = wg.shape[1]
    chunk = dff // 2 if (dff // 2) % LANES == 0 else dff
    row = pl.BlockSpec((tm, d), lambda i: (i, 0))
    return pl.pallas_call(
        functools.partial(_ffn_body, chunk=chunk),
        grid=(t // tm,),
        in_specs=[row, _resident(wg.shape), _resident(wu.shape), _resident(wd.shape), _resident(g.shape),
                  _resident(bias.shape)],
        out_specs=row,
        out_shape=jax.ShapeDtypeStruct((t, d), F32),
        compiler_params=_cparams("arbitrary"),
        name="ffn",
    )(x2d, wg, wu, wd, g, bias)


def _tile(n, pref):
    t = min(n, pref)
    assert n % t == 0, (n, pref)
    return t


def _layer(x, mem, positions, w_in, conv_w, a_log, dt_bias, gdn_norm_w, cmp_pos_k, cmp_pos_v, cmp_k_w1, cmp_k_w2,
           cmp_v_w1, cmp_v_w2, w_up_gdn, w_up_nsa, w_mix_out, ln1_g, ln1_b, xa_wq, xa_wk, xa_wv, xa_wo, ln2_g,
           ln2_b, ffn_w_gate, ffn_w_up, ffn_w_down, ln3_g, ln3_b):
    b, s, d = x.shape
    t = b * s
    assert d == D_MODEL and s % WINDOW == 0 and s // SLC_BLOCK <= LANES and s >= 2 * WINDOW

    o_qkv, o_z = 0, 3 * GDN_QK
    o_a = o_z + GDN_V
    o_b = o_a + GDN_HEADS
    o_qn = o_b + GDN_HEADS
    o_kv6 = o_qn + NSA_Q
    o_gn = o_kv6 + 6 * HEAD_DIM
    o_ga = o_gn + 3 * NSA_HEADS
    o_gb = o_ga + D_MODEL
    w_main = jnp.concatenate([w_in[:, o_ga:o_gb], w_in[:, o_gb:o_gb + D_MODEL], w_in[:, o_qkv:o_z], w_in[:, o_z:o_a],
                              w_in[:, o_qn:o_kv6], w_in[:, o_kv6:o_gn]], axis=1).astype(BF16)
    n_small = 2 * GDN_HEADS + 3 * NSA_HEADS
    w_small = jnp.concatenate([w_in[:, o_a:o_qn], w_in[:, o_gn:o_ga], jnp.zeros((d, LANES - n_small), w_in.dtype)],
                              axis=1).astype(BF16)

    x2d = x.reshape(t, d)
    proj2d, small2d = _project(x2d, w_main, w_small, _tile(t, 512))
    proj3 = proj2d.reshape(b, s, N_MAIN)
    small3 = small2d.reshape(b, s, LANES)

    hp = jnp.zeros((8, LANES), F32).at[0, :GDN_HEADS].set(a_log.astype(F32)).at[1, :GDN_HEADS].set(
        dt_bias.astype(F32))
    o_gdn = _gdn(proj3, small3, conv_w.astype(F32), hp, gdn_norm_w.reshape(1, HEAD_DIM).astype(F32), _tile(s, 1024))

    half = HEAD_DIM // 2
    inv = ROPE_THETA ** (-jnp.arange(half, dtype=F32) / half)
    inv_row = jnp.concatenate([inv, inv]).reshape(1, HEAD_DIM)
    posf = positions.astype(F32)
    tq = _tile(s, 256)
    q_t, ks_aug, vs_t, kw_rope, vw_t = _nsa_prep(proj3, jnp.broadcast_to(posf[:, :, None], (b, s, LANES)), inv_row,
                                                 _tile(s, 1024), tq)

    n_seg = s // CMP_STRIDE
    seg_w = CMP_STRIDE * HEAD_DIM
    kc_seg = proj3[:, :, COL_KV6:COL_KV6 + HEAD_DIM].reshape(b, n_seg, seg_w)
    vc_seg = proj3[:, :, COL_KV6 + HEAD_DIM:COL_KV6 + 2 * HEAD_DIM].reshape(b, n_seg, seg_w)
    posc = posf[:, CMP_LEN - 1::CMP_STRIDE]
    posc = jnp.concatenate([posc, posc[:, -1:]], axis=1).reshape(b, n_seg, 1)
    k_cmp, v_cmp_t = _compress(kc_seg, vc_seg, cmp_pos_k.reshape(1, CMP_LEN * HEAD_DIM).astype(F32),
                             cmp_pos_v.reshape(1, CMP_LEN * HEAD_DIM).astype(F32), cmp_k_w1.astype(BF16),
                             cmp_k_w2.astype(BF16), cmp_v_w1.astype(BF16), cmp_v_w2.T.astype(BF16), posc, inv_row)
    o_nsa = _nsa_attention(q_t, small3, k_cmp, v_cmp_t, ks_aug, vs_t, kw_rope, vw_t, tq)

    vec = lambda p: p.reshape(1, d).astype(F32)
    x1 = _merge(x2d, o_gdn.reshape(t, GDN_V), o_nsa.reshape(t, NSA_Q), proj2d, w_up_gdn.astype(BF16),
                w_up_nsa.astype(BF16), w_mix_out.astype(BF16), vec(ln1_g), vec(ln1_b), _tile(t, 1024))
    mlen = mem.shape[1]
    mk, mv = _memkv(mem.reshape(b * mlen, d), xa_wk.astype(BF16), xa_wv.astype(BF16), _tile(b * mlen, 512))
    x2 = _xattn(x1.reshape(b, s, d), mk.reshape(b, mlen, d), mv.reshape(b, mlen, d), xa_wq.astype(BF16),
                xa_wo.astype(BF16), vec(ln2_g), vec(ln2_b), _tile(s, 1024))
    x3 = _ffn(x2.reshape(t, d), ffn_w_gate.astype(BF16), ffn_w_up.astype(BF16), ffn_w_down.astype(BF16),
              vec(ln3_g), vec(ln3_b), _tile(t, 512))
    return x3.reshape(b, s, d)


def kernel(x, mem, positions, w_in, gdn_conv_w, gdn_a_log, gdn_dt_bias, gdn_norm_w, cmp_pos_k, cmp_pos_v, cmp_k_w1, cmp_k_w2, cmp_v_w1, cmp_v_w2, w_up_gdn, w_up_nsa, w_mix_out, ln1_g, ln1_b, xa_wq, xa_wk, xa_wv, xa_wo, ln2_g, ln2_b, ffn_w_gate, ffn_w_up, ffn_w_down, ln3_g, ln3_b):
    for l in range(w_in.shape[0]):
        x = _layer(x, mem, positions, w_in[l], gdn_conv_w[l], gdn_a_log[l], gdn_dt_bias[l], gdn_norm_w[l],
                   cmp_pos_k[l], cmp_pos_v[l], cmp_k_w1[l], cmp_k_w2[l], cmp_v_w1[l], cmp_v_w2[l], w_up_gdn[l],
                   w_up_nsa[l], w_mix_out[l], ln1_g[l], ln1_b[l], xa_wq[l], xa_wk[l], xa_wv[l], xa_wo[l], ln2_g[l],
                   ln2_b[l], ffn_w_gate[l], ffn_w_up[l], ffn_w_down[l], ln3_g[l], ln3_b[l])
    return x
```
